```python
import jax
import jax.numpy as jnp
from jax import lax
import numpy as np

D_MODEL = 1024
BATCH = 2
SEQ = 8192
DEPTH = 1
DEC_BATCH = 128
DEC_SEQ = 8
PAST_LEN = 8192
PAGE_SIZE = 128

N_HEADS = 16
HEAD_DIM = 64
N_KV = 4
GROUP = N_HEADS // N_KV
CMP_STRIDE = 16
CMP_BLOCK = 2 * CMP_STRIDE
CMP_HIDDEN = 64
SEL_BLOCK = 64
N_SEL = 16
WINDOW = 512
Q_BLOCK = 128
CHUNK = 128
A_GROUPS = 8
A_HALF = D_MODEL // 2
A_GDIM = A_HALF // A_GROUPS
N_GROUPS = 4
EXPERTS_PER_GROUP = 8
N_EXPERTS = N_GROUPS * EXPERTS_PER_GROUP
TOP_K = 2
D_EXPERT = D_MODEL // 2
MOE_BLOCK = 128
LN_EPS = 1e-5
ALPHA = (2.0 * DEPTH) ** 0.25
BETA = (8.0 * DEPTH) ** -0.25
NEG = -1e30
FORCE = 1e9
Q_COLS = N_HEADS * HEAD_DIM
KV_COLS = N_KV * HEAD_DIM
SPLIT_SIZES = (Q_COLS, KV_COLS, KV_COLS, KV_COLS, KV_COLS, KV_COLS, KV_COLS, 3 * N_HEADS, A_HALF, A_HALF, D_MODEL, D_MODEL)
IN_COLS = sum(SPLIT_SIZES)

kernel_name = 'hybrid_gmlp_nsa_hmoe_step'


def layer_norm(x, g, b):
    xf = x.astype(jnp.float32)
    mu = jnp.mean(xf, axis=-1, keepdims=True)
    var = jnp.mean(jnp.square(xf - mu), axis=-1, keepdims=True)
    return ((xf - mu) * lax.rsqrt(var + LN_EPS)).astype(x.dtype) * g + b


def masked_softmax(s, mask):
    s = jnp.where(mask, s.astype(jnp.float32), NEG)
    e = jnp.exp(s - jnp.max(s, axis=-1, keepdims=True)) * mask
    return e / jnp.maximum(jnp.sum(e, axis=-1, keepdims=True), 1e-30)


def split_proj(x, w_in):
    b, t = x.shape[:2]
    offs = [int(o) for o in np.cumsum(SPLIT_SIZES)[:-1]]
    q, kc, vc, ks, vs, kw, vw, g_nsa, a_u, a_v, g_a, g_b = jnp.split(x @ w_in, offs, axis=-1)
    hq = lambda z: z.reshape(b, t, N_HEADS, HEAD_DIM)
    hk = lambda z: z.reshape(b, t, N_KV, HEAD_DIM)
    gates = jax.nn.sigmoid(g_nsa).reshape(b, t, N_HEADS, 3)
    return (hq(q), hk(kc), hk(vc), hk(ks), hk(vs), hk(kw), hk(vw), gates, a_u, a_v, g_a, g_b)


def compress_rows(rows, pe, w1, b1, w2, b2):
    n_sub = rows.shape[0] // CMP_STRIDE
    sub = rows.reshape(n_sub, CMP_STRIDE, N_KV, HEAD_DIM)
    pe2 = pe.reshape(2, CMP_STRIDE, 1, HEAD_DIM)
    w1r = w1.reshape(2, CMP_STRIDE, HEAD_DIM, CMP_HIDDEN)
    first = jnp.einsum('nsgd,sdh->ngh', sub + pe2[0], w1r[0])
    second = jnp.einsum('nsgd,sdh->ngh', sub + pe2[1], w1r[1])
    hid = jax.nn.gelu(first[:-1] + second[1:] + b1)
    return jnp.einsum('ngh,hd->ngd', hid, w2) + b2


def nsa_core(q, q_pos, kc, vc, ks_blk, vs_blk, kw, vw, kw_pos, gates):
    tq = q.shape[0]
    qg = q.reshape(tq, N_KV, GROUP, HEAD_DIM) * (HEAD_DIM ** -0.5)
    n_cmp = kc.shape[0]
    cmp_end = jnp.arange(n_cmp) * CMP_STRIDE + (CMP_BLOCK - 1)
    mask_c = (cmp_end[None, :] <= q_pos[:, None])[:, None, None, :]
    p_c = masked_softmax(jnp.einsum('tgrd,ngd->tgrn', qg, kc), mask_c)
    o_c = jnp.einsum('tgrn,ngd->tgrd', p_c.astype(vc.dtype), vc)
    n_sel = ks_blk.shape[0]
    c_start = jnp.arange(n_cmp)[:, None] * CMP_STRIDE
    s_start = jnp.arange(n_sel)[None, :] * SEL_BLOCK
    overlap = ((c_start < s_start + SEL_BLOCK) & (c_start + CMP_BLOCK > s_start)).astype(jnp.float32)
    imp = jnp.einsum('tgn,nj->tgj', jnp.sum(p_c, axis=2), overlap)
    blk = jnp.arange(n_sel)[None, :]
    cur = (q_pos // SEL_BLOCK)[:, None]
    forced = (blk == 0) | (blk == cur) | (blk == cur - 1)
    imp = jnp.where(forced[:, None, :], FORCE, imp)
    imp = jnp.where((blk <= cur)[:, None, :], imp, NEG)
    n_top = min(N_SEL, n_sel)
    _, idx = lax.top_k(imp, n_top)
    g_idx = jnp.arange(N_KV)[None, :, None]
    k_sel = ks_blk.transpose(2, 0, 1, 3)[g_idx, idx].reshape(tq, N_KV, n_top * SEL_BLOCK, HEAD_DIM)
    v_sel = vs_blk.transpose(2, 0, 1, 3)[g_idx, idx].reshape(tq, N_KV, n_top * SEL_BLOCK, HEAD_DIM)
    sel_pos = (idx[..., None] * SEL_BLOCK + jnp.arange(SEL_BLOCK)).reshape(tq, N_KV, n_top * SEL_BLOCK)
    mask_s = (sel_pos <= q_pos[:, None, None])[:, :, None, :]
    p_s = masked_softmax(jnp.einsum('tgrd,tgnd->tgrn', qg, k_sel), mask_s)
    o_s = jnp.einsum('tgrn,tgnd->tgrd', p_s.astype(v_sel.dtype), v_sel)
    dist = q_pos[:, None] - kw_pos[None, :]
    mask_w = ((dist >= 0) & (dist < WINDOW) & (kw_pos[None, :] >= 0))[:, None, None, :]
    p_w = masked_softmax(jnp.einsum('tgrd,lgd->tgrl', qg, kw), mask_w)
    o_w = jnp.einsum('tgrl,lgd->tgrd', p_w.astype(vw.dtype), vw)
    g = gates.reshape(tq, N_KV, GROUP, 3, 1)
    o = g[..., 0, :] * o_c + g[..., 1, :] * o_s + g[..., 2, :] * o_w
    return o.reshape(tq, N_HEADS * HEAD_DIM)


def nsa_prompt(q, kc, vc, ks, vs, kw, vw, gates, comp):
    pe, w1, b1, w2, b2 = comp
    b, t = q.shape[:2]
    cmp = jax.vmap(compress_rows, in_axes=(0, None, None, None, None, None))
    kc_c = cmp(kc, pe[0], w1[0], b1[0], w2[0], b2[0])
    vc_c = cmp(vc, pe[1], w1[1], b1[1], w2[1], b2[1])
    ks_b = ks.reshape(b, t // SEL_BLOCK, SEL_BLOCK, N_KV, HEAD_DIM)
    vs_b = vs.reshape(b, t // SEL_BLOCK, SEL_BLOCK, N_KV, HEAD_DIM)
    kw_p = jnp.pad(kw, ((0, 0), (WINDOW, 0), (0, 0), (0, 0)))
    vw_p = jnp.pad(vw, ((0, 0), (WINDOW, 0), (0, 0), (0, 0)))
    core = jax.vmap(nsa_core, in_axes=(0, None, 0, 0, 0, 0, 0, 0, None, 0))

    def q_block(i):
        s = i * Q_BLOCK
        qb = lax.dynamic_slice_in_dim(q, s, Q_BLOCK, axis=1)
        gb = lax.dynamic_slice_in_dim(gates, s, Q_BLOCK, axis=1)
        kwb = lax.dynamic_slice_in_dim(kw_p, s, WINDOW + Q_BLOCK, axis=1)
        vwb = lax.dynamic_slice_in_dim(vw_p, s, WINDOW + Q_BLOCK, axis=1)
        q_pos = s + jnp.arange(Q_BLOCK)
        kw_pos = s - WINDOW + jnp.arange(WINDOW + Q_BLOCK)
        return core(qb, q_pos, kc_c, vc_c, ks_b, vs_b, kwb, vwb, kw_pos, gb)

    o = lax.map(q_block, jnp.arange(t // Q_BLOCK))
    return jnp.swapaxes(o, 0, 1).reshape(b, t, N_HEADS * HEAD_DIM)


def nsa_sample(l, q, kc, vc, ks, vs, kw, vw, gates, cache_cmp_k, cache_cmp_v, cache_sel_k, cache_sel_v, win_k, win_v, page_table, comp):
    pe, w1, b1, w2, b2 = comp
    n_new = q.shape[1]
    pad = (-n_new) % SEL_BLOCK
    win_buf = win_k.shape[1]
    past_len = page_table.shape[1] * PAGE_SIZE
    q_pos = past_len + jnp.arange(n_new)
    kw_pos = past_len - win_buf + jnp.arange(win_buf + n_new)

    def rows(pool, pages, new):
        past = pool[l, pages].reshape(past_len, N_KV, HEAD_DIM)
        return jnp.concatenate([past, new, jnp.zeros((pad, N_KV, HEAD_DIM), new.dtype)], axis=0)

    def one_seq(args):
        qb, gb, kcb, vcb, ksb, vsb, kwb, vwb, wkb, wvb, pages = args
        kc_c = compress_rows(rows(cache_cmp_k, pages, kcb), pe[0], w1[0], b1[0], w2[0], b2[0])
        vc_c = compress_rows(rows(cache_cmp_v, pages, vcb), pe[1], w1[1], b1[1], w2[1], b2[1])
        ks_b = rows(cache_sel_k, pages, ksb).reshape(-1, SEL_BLOCK, N_KV, HEAD_DIM)
        vs_b = rows(cache_sel_v, pages, vsb).reshape(-1, SEL_BLOCK, N_KV, HEAD_DIM)
        kw_all = jnp.concatenate([wkb, kwb], axis=0)
        vw_all = jnp.concatenate([wvb, vwb], axis=0)
        return nsa_core(qb, q_pos, kc_c, vc_c, ks_b, vs_b, kw_all, vw_all, kw_pos, gb)

    return lax.map(one_seq, (q, gates, kc, vc, ks, vs, kw, vw, win_k, win_v, page_table))


def spatial_gate(u, v, ln_g, ln_b, w_s, b_s):
    lc = u.shape[2]
    v = layer_norm(jax.nn.gelu(v), ln_g, ln_b)
    vg = v.reshape(v.shape[:3] + (A_GROUPS, A_GDIM))
    w = jnp.tril(w_s[:, :lc, :lc])
    mixed = jnp.einsum('gts,bcsgd->bctgd', w, vg) + b_s[:, :lc].T[None, None, :, :, None]
    return jax.nn.gelu(u) * mixed.reshape(u.shape), v


def gated_merge(x, o_nsa, a_out, g_a, g_b, w_proj_a, w_proj_b, w_out, ln_g, ln_b):
    m = jax.nn.sigmoid(g_a) * (a_out @ w_proj_a) + jax.nn.sigmoid(g_b) * (o_nsa @ w_proj_b)
    return layer_norm(ALPHA * x + m @ w_out, ln_g, ln_b)


def hier_moe(x, rg_w, rg_b, re_w, re_b, w_gate, w_up, w_down):
    shp = x.shape
    xt = x.reshape(-1, D_MODEL)
    t = xt.shape[0]
    lg = (xt @ rg_w + rg_b).astype(jnp.float32)
    grp = jnp.argmax(lg, axis=-1)
    p_grp = jnp.max(jax.nn.softmax(lg, axis=-1), axis=-1)
    le = (xt @ re_w + re_b).astype(jnp.float32).reshape(t, N_GROUPS, EXPERTS_PER_GROUP)
    le_g = le[jnp.arange(t), grp]
    top_l, top_i = lax.top_k(le_g, TOP_K)
    w_top = jax.nn.softmax(top_l, axis=-1) * p_grp[:, None]
    expert = grp[:, None] * EXPERTS_PER_GROUP + top_i
    n_assign = t * TOP_K
    flat_e = expert.reshape(-1)
    flat_tok = jnp.arange(n_assign) // TOP_K
    flat_w = w_top.reshape(-1)
    order = jnp.argsort(flat_e)
    e_s, tok_s, w_s = flat_e[order], flat_tok[order], flat_w[order]
    counts = jnp.zeros((N_EXPERTS,), jnp.int32).at[flat_e].add(1)
    padded = (counts + MOE_BLOCK - 1) // MOE_BLOCK * MOE_BLOCK
    pad_end = jnp.cumsum(padded)
    pad_start = pad_end - padded
    start = jnp.cumsum(counts) - counts
    dest = pad_start[e_s] + jnp.arange(n_assign) - start[e_s]
    n_blocks = -(-n_assign // MOE_BLOCK) + N_EXPERTS
    n_slots = n_blocks * MOE_BLOCK
    slot_tok = jnp.zeros((n_slots,), jnp.int32).at[dest].set(tok_s)
    slot_w = jnp.zeros((n_slots,), jnp.float32).at[dest].set(w_s)
    blk_start = jnp.arange(n_blocks) * MOE_BLOCK
    blk_exp = jnp.minimum(jnp.sum(pad_end[None, :] <= blk_start[:, None], axis=1), N_EXPERTS - 1)

    def run_block(args):
        e, toks = args
        xb = xt[toks]
        return (jax.nn.silu(xb @ w_gate[e]) * (xb @ w_up[e])) @ w_down[e]

    out = lax.map(run_block, (blk_exp, slot_tok.reshape(n_blocks, MOE_BLOCK))).reshape(n_slots, D_MODEL)
    y = jnp.zeros_like(xt).at[slot_tok].add(out * slot_w[:, None].astype(out.dtype))
    return y.reshape(shp)


def decoder_layer(l, x_p, x_s, cache_cmp_k, cache_cmp_v, cache_sel_k, cache_sel_v, win_k, win_v, page_table,
                  w_in, comp, sgu_ln_g, sgu_ln_b, sgu_w, sgu_b, w_proj_a, w_proj_b, w_out, ln1_g, ln1_b, moe, ln2_g, ln2_b):
    q, kc, vc, ks, vs, kw, vw, gates, a_u, a_v, g_a, g_b = split_proj(x_p, w_in)
    b, t = x_p.shape[:2]
    o_nsa = nsa_prompt(q, kc, vc, ks, vs, kw, vw, gates, comp)
    a_out, _ = spatial_gate(a_u.reshape(b, t // CHUNK, CHUNK, A_HALF), a_v.reshape(b, t // CHUNK, CHUNK, A_HALF),
                            sgu_ln_g, sgu_ln_b, sgu_w, sgu_b)
    h_p = gated_merge(x_p, o_nsa, a_out.reshape(b, t, A_HALF), g_a, g_b, w_proj_a, w_proj_b, w_out, ln1_g, ln1_b)
    y_p = layer_norm(ALPHA * h_p + hier_moe(h_p, *moe), ln2_g, ln2_b)
    keep_p = min(WINDOW, t)
    prompt_state = (kc, vc, ks, vs, kw[:, t - keep_p:], vw[:, t - keep_p:])
    q, kc, vc, ks, vs, kw, vw, gates, a_u, a_v, g_a, g_b = split_proj(x_s, w_in)
    bs, ts = x_s.shape[:2]
    o_nsa = nsa_sample(l, q, kc, vc, ks, vs, kw, vw, gates, cache_cmp_k, cache_cmp_v, cache_sel_k, cache_sel_v,
                       win_k, win_v, page_table, comp)
    a_out, v_rows = spatial_gate(a_u.reshape(bs, 1, ts, A_HALF), a_v.reshape(bs, 1, ts, A_HALF),
                                 sgu_ln_g, sgu_ln_b, sgu_w, sgu_b)
    h_s = gated_merge(x_s, o_nsa, a_out.reshape(bs, ts, A_HALF), g_a, g_b, w_proj_a, w_proj_b, w_out, ln1_g, ln1_b)
    y_s = layer_norm(ALPHA * h_s + hier_moe(h_s, *moe), ln2_g, ln2_b)
    wk_all = jnp.concatenate([win_k, kw], axis=1)
    wv_all = jnp.concatenate([win_v, vw], axis=1)
    keep_s = min(WINDOW, wk_all.shape[1])
    sample_state = (kc, vc, ks, vs, wk_all[:, wk_all.shape[1] - keep_s:], wv_all[:, wv_all.shape[1] - keep_s:],
                    v_rows.reshape(bs, ts, A_HALF))
    return y_p, y_s, prompt_state + sample_state


def setup_inputs(seed: int = 0) -> dict:
    key = jax.random.key(seed)
    k = jax.random.split(key, 40)
    f32 = jnp.float32

    def nrm(kk, shape, scale=1.0):
        return jax.random.normal(kk, shape, f32) * scale

    n_pages = PAST_LEN // PAGE_SIZE
    n_pool = (5 * DEC_BATCH * n_pages) // 4
    win_buf = min(WINDOW, PAST_LEN)
    pool_shape = (DEPTH, n_pool, PAGE_SIZE, N_KV, HEAD_DIM)
    win_shape = (DEPTH, DEC_BATCH, win_buf, N_KV, HEAD_DIM)
    page_table = jax.random.permutation(k[8], n_pool)[: DEC_BATCH * n_pages].reshape(DEC_BATCH, n_pages).astype(jnp.int32)
    col_scale = np.concatenate([np.ones(Q_COLS),
                                np.tile(np.concatenate([np.ones(KV_COLS), np.full(KV_COLS, BETA)]), 3),
                                np.ones(IN_COLS - Q_COLS - 6 * KV_COLS)]).astype(np.float32)
    return {
        'x_prompt': nrm(k[0], (BATCH, SEQ, D_MODEL)),
        'x_sample': nrm(k[1], (DEC_BATCH, DEC_SEQ, D_MODEL)),
        'cache_cmp_k': nrm(k[2], pool_shape),
        'cache_cmp_v': nrm(k[3], pool_shape),
        'cache_sel_k': nrm(k[4], pool_shape),
        'cache_sel_v': nrm(k[5], pool_shape),
        'cache_win_k': nrm(k[6], win_shape),
        'cache_win_v': nrm(k[7], win_shape),
        'page_table': page_table,
        'w_in': nrm(k[9], (DEPTH, D_MODEL, IN_COLS), D_MODEL ** -0.5) * jnp.asarray(col_scale),
        'cmp_pe': nrm(k[10], (DEPTH, 2, CMP_BLOCK, HEAD_DIM), 0.1),
        'cmp_w1': nrm(k[11], (DEPTH, 2, CMP_BLOCK, HEAD_DIM, CMP_HIDDEN), (CMP_BLOCK * HEAD_DIM) ** -0.5),
        'cmp_b1': nrm(k[12], (DEPTH, 2, CMP_HIDDEN), 0.01),
        'cmp_w2': nrm(k[13], (DEPTH, 2, CMP_HIDDEN, HEAD_DIM), CMP_HIDDEN ** -0.5),
        'cmp_b2': nrm(k[14], (DEPTH, 2, HEAD_DIM), 0.01),
        'sgu_ln_g': 1.0 + nrm(k[15], (DEPTH, A_HALF), 0.01),
        'sgu_ln_b': nrm(k[16], (DEPTH, A_HALF), 0.01),
        'sgu_w': nrm(k[17], (DEPTH, A_GROUPS, CHUNK, CHUNK), CHUNK ** -0.5),
        'sgu_b': 1.0 + nrm(k[18], (DEPTH, A_GROUPS, CHUNK), 0.01),
        'w_proj_a': nrm(k[19], (DEPTH, A_HALF, D_MODEL), BETA * A_HALF ** -0.5),
        'w_proj_b': nrm(k[20], (DEPTH, Q_COLS, D_MODEL), BETA * Q_COLS ** -0.5),
        'w_out': nrm(k[21], (DEPTH, D_MODEL, D_MODEL), BETA * D_MODEL ** -0.5),
        'ln1_g': 1.0 + nrm(k[22], (DEPTH, D_MODEL), 0.01),
        'ln1_b': nrm(k[23], (DEPTH, D_MODEL), 0.01),
        'router_group_w': nrm(k[24], (DEPTH, D_MODEL, N_GROUPS), D_MODEL ** -0.5),
        'router_group_b': nrm(k[25], (DEPTH, N_GROUPS), 0.01),
        'router_expert_w': nrm(k[26], (DEPTH, D_MODEL, N_EXPERTS), D_MODEL ** -0.5),
        'router_expert_b': nrm(k[27], (DEPTH, N_EXPERTS), 0.01),
        'exp_w_gate': nrm(k[28], (DEPTH, N_EXPERTS, D_MODEL, D_EXPERT), D_MODEL ** -0.5),
        'exp_w_up': nrm(k[29], (DEPTH, N_EXPERTS, D_MODEL, D_EXPERT), D_MODEL ** -0.5),
        'exp_w_down': nrm(k[30], (DEPTH, N_EXPERTS, D_EXPERT, D_MODEL), BETA * D_EXPERT ** -0.5),
        'ln2_g': 1.0 + nrm(k[31], (DEPTH, D_MODEL), 0.01),
        'ln2_b': nrm(k[32], (DEPTH, D_MODEL), 0.01),
    }


def reference(x_prompt, x_sample, cache_cmp_k, cache_cmp_v, cache_sel_k, cache_sel_v, cache_win_k, cache_win_v,
              page_table, w_in, cmp_pe, cmp_w1, cmp_b1, cmp_w2, cmp_b2, sgu_ln_g, sgu_ln_b, sgu_w, sgu_b,
              w_proj_a, w_proj_b, w_out, ln1_g, ln1_b, router_group_w, router_group_b, router_expert_w,
              router_expert_b, exp_w_gate, exp_w_up, exp_w_down, ln2_g, ln2_b):
    x_p, x_s = x_prompt, x_sample
    states = []
    for l in range(DEPTH):
        x_p, x_s, st = decoder_layer(
            l, x_p, x_s, cache_cmp_k, cache_cmp_v, cache_sel_k, cache_sel_v, cache_win_k[l], cache_win_v[l], page_table,
            w_in[l], (cmp_pe[l], cmp_w1[l], cmp_b1[l], cmp_w2[l], cmp_b2[l]),
            sgu_ln_g[l], sgu_ln_b[l], sgu_w[l], sgu_b[l], w_proj_a[l], w_proj_b[l], w_out[l], ln1_g[l], ln1_b[l],
            (router_group_w[l], router_group_b[l], router_expert_w[l], router_expert_b[l],
             exp_w_gate[l], exp_w_up[l], exp_w_down[l]),
            ln2_g[l], ln2_b[l])
        states.append(st)
    (p_kc, p_vc, p_ks, p_vs, p_kw, p_vw,
     s_kc, s_vc, s_ks, s_vs, s_kw, s_vw, s_chunk_v) = [jnp.stack(z) for z in zip(*states)]
    return (x_p, x_s, p_kc, p_vc, p_ks, p_vs, p_kw, p_vw, s_kc, s_vc, s_ks, s_vs, s_kw, s_vw, s_chunk_v)
```

```python
import functools

import numpy as np
import jax
import jax.numpy as jnp
from jax import lax
from jax.experimental import pallas as pl
from jax.experimental.pallas import tpu as pltpu

F32 = jnp.float32
BF16 = jnp.bfloat16

D_MODEL = 1024
N_HEADS = 16
HEAD_DIM = 64
N_KV = 4
GROUP = N_HEADS // N_KV
KV_COLS = N_KV * HEAD_DIM
CMP_STRIDE = 16
CMP_BLOCK = 32
CMP_HIDDEN = 64
SEL_BLOCK = 64
N_SEL = 16
WINDOW = 512
Q_BLOCK = 128
PAGE_SIZE = 128
CHUNK = 128
A_GROUPS = 8
A_HALF = D_MODEL // 2
A_GDIM = A_HALF // A_GROUPS
N_GROUPS = 4
EXPERTS_PER_GROUP = 8
N_EXPERTS = N_GROUPS * EXPERTS_PER_GROUP
TOP_K = 2
D_EXPERT = D_MODEL // 2
LN_EPS = 1e-5
NEG = -1e30
FORCE = 1e9
BELOW_NEG = -3e38

LANES = 128
VMEM_LIMIT = 56 * 1024 * 1024

INPROJ_TM = 256
MERGE_TM = 256
MOE_BM = 256
SEL_TK = 512


def _gelu(x):
    return jax.nn.gelu(x, approximate=True)


def _layer_norm_rows(x, g, b):
    mu = jnp.mean(x, axis=-1, keepdims=True)
    xc = x - mu
    var = jnp.mean(xc * xc, axis=-1, keepdims=True)
    return xc * lax.rsqrt(var + LN_EPS) * g + b


def _params(*sem):
    return pltpu.CompilerParams(dimension_semantics=tuple(sem), vmem_limit_bytes=VMEM_LIMIT)


_C_Q = 0
_C_KV = D_MODEL
_C_AU = _C_KV + 6 * KV_COLS
_C_AV = _C_AU + A_HALF
_C_GA = _C_AV + A_HALF
_C_GB = _C_GA + D_MODEL
_C_GATE = _C_GB + D_MODEL
_C_END = _C_GATE + N_KV * LANES


def _pack_w_in(w_in):
    q_cols = N_HEADS * HEAD_DIM
    o_gate = q_cols + 6 * KV_COLS
    o_au = o_gate + 3 * N_HEADS
    src = np.zeros((N_KV * LANES,), np.int32)
    valid = np.zeros((N_KV * LANES,), np.float32)
    for g in range(N_KV):
        for c in range(3):
            for r in range(GROUP):
                src[g * LANES + c * GROUP + r] = o_gate + (g * GROUP + r) * 3 + c
                valid[g * LANES + c * GROUP + r] = 1.0
    w_gate = w_in[:, src] * jnp.asarray(valid)
    packed = jnp.concatenate([w_in[:, :o_gate], w_in[:, o_au:], w_gate], axis=1)
    return packed.astype(BF16)


def _inproj_body(x_ref, w_ref, lng_ref, lnb_ref,
                 q_ref, kc_ref, vc_ref, ks_ref, vs_ref, kw_ref, vw_ref,
                 ksb_ref, vsb_ref, kwb_ref, vwb_ref,
                 gu_ref, v_ref, sga_ref, sgb_ref, gates_ref):
    x = x_ref[...].astype(BF16)

    def mm(c0, n):
        return jnp.dot(x, w_ref[:, c0:c0 + n], preferred_element_type=F32)

    q_ref[...] = (mm(_C_Q, D_MODEL) * (HEAD_DIM ** -0.5)).astype(BF16)
    f32_refs = (kc_ref, vc_ref, ks_ref, vs_ref, kw_ref, vw_ref)
    bf_refs = (None, None, ksb_ref, vsb_ref, kwb_ref, vwb_ref)
    for i in range(6):
        y = mm(_C_KV + i * KV_COLS, KV_COLS)
        f32_refs[i][...] = y
        if bf_refs[i] is not None:
            bf_refs[i][...] = y.astype(BF16)
    gu_ref[...] = _gelu(mm(_C_AU, A_HALF))
    v_ref[...] = _layer_norm_rows(_gelu(mm(_C_AV, A_HALF)), lng_ref[...], lnb_ref[...])
    sga_ref[...] = jax.nn.sigmoid(mm(_C_GA, D_MODEL))
    sgb_ref[...] = jax.nn.sigmoid(mm(_C_GB, D_MODEL))
    gates_ref[...] = jax.nn.sigmoid(mm(_C_GATE, N_KV * LANES))


def _in_proj(x, w_packed, ln_g, ln_b):
    t = x.shape[0]
    tm = INPROJ_TM
    assert t % tm == 0
    row = lambda n: pl.BlockSpec((tm, n), lambda i: (i, 0))
    full = lambda a: pl.BlockSpec(a.shape, lambda i: (0,) * a.ndim)
    out_shapes = (
        [jax.ShapeDtypeStruct((t, D_MODEL), BF16)]
        + [jax.ShapeDtypeStruct((t, KV_COLS), F32)] * 6
        + [jax.ShapeDtypeStruct((t, KV_COLS), BF16)] * 4
        + [jax.ShapeDtypeStruct((t, A_HALF), F32)] * 2
        + [jax.ShapeDtypeStruct((t, D_MODEL), F32)] * 2
        + [jax.ShapeDtypeStruct((t, N_KV * LANES), F32)]
    )
    out_specs = [row(s.shape[1]) for s in out_shapes]
    return pl.pallas_call(
        _inproj_body,
        grid=(t // tm,),
        in_specs=[row(D_MODEL), full(w_packed), full(ln_g), full(ln_b)],
        out_specs=out_specs,
        out_shape=out_shapes,
        compiler_params=_params("parallel"),
        name="in_proj",
    )(x, w_packed, ln_g, ln_b)


def _compress_weights(pe, w1, b1, w2, b2):
    eye = jnp.eye(N_KV, dtype=F32)
    w1r = w1.reshape(2, 2, CMP_STRIDE, HEAD_DIM, CMP_HIDDEN)
    w1bd = jnp.einsum('kfsdh,gj->kfsgdjh', w1r, eye).reshape(
        2, 2, CMP_STRIDE * KV_COLS, N_KV * CMP_HIDDEN).astype(BF16)
    per = pe.reshape(2, 2, CMP_STRIDE, 1, HEAD_DIM)
    pe_row = jnp.broadcast_to(per, (2, 2, CMP_STRIDE, N_KV, HEAD_DIM)).reshape(2, 2, 1, CMP_STRIDE * KV_COLS)
    b1_row = jnp.tile(b1, (1, N_KV)).reshape(2, 1, N_KV * CMP_HIDDEN)
    w2bd = jnp.einsum('khd,gj->kghjd', w2, eye).reshape(2, N_KV * CMP_HIDDEN, KV_COLS).astype(BF16)
    b2_row = jnp.tile(b2, (1, N_KV)).reshape(2, 1, KV_COLS)
    return pe_row, w1bd, b1_row, w2bd, b2_row


def _compress_body(xk_ref, xv_ref, pe_ref, w1_ref, b1_ref, w2_ref, b2_ref, ok_ref, ov_ref,
                   first_ref, second_ref, *, rows, tile, out_rows):
    j = pl.program_id(1)
    r0 = pl.multiple_of(j * tile, 8)
    for kv, x_ref in enumerate((xk_ref, xv_ref)):
        x = x_ref[0]
        first_ref[kv, pl.ds(r0, tile), :] = jnp.dot(
            (x + pe_ref[kv, 0]).astype(BF16), w1_ref[kv, 0], preferred_element_type=F32)
        second_ref[kv, pl.ds(r0, tile), :] = jnp.dot(
            (x + pe_ref[kv, 1]).astype(BF16), w1_ref[kv, 1], preferred_element_type=F32)

    @pl.when(j == pl.num_programs(1) - 1)
    def _():
        for kv, o_ref in enumerate((ok_ref, ov_ref)):
            nxt = pltpu.roll(second_ref[kv], rows - 1, axis=0)
            hid = _gelu(first_ref[kv] + nxt + b1_ref[kv])
            out = jnp.dot(hid.astype(BF16), w2_ref[kv], preferred_element_type=F32) + b2_ref[kv]
            o_ref[0, 0:rows, :] = out.astype(BF16)
            if out_rows > rows:
                o_ref[0, rows:out_rows, :] = jnp.zeros((out_rows - rows, KV_COLS), BF16)


def _compress(xk, xv, cw, *, tile, out_rows):
    pe_row, w1bd, b1_row, w2bd, b2_row = cw
    n, rows, width = xk.shape
    assert rows % tile == 0 and tile % 8 == 0
    xspec = pl.BlockSpec((1, tile, width), lambda i, j: (i, j, 0))
    full = lambda a: pl.BlockSpec(a.shape, lambda i, j: (0,) * a.ndim)
    ospec = pl.BlockSpec((1, out_rows, KV_COLS), lambda i, j: (i, 0, 0))
    oshape = jax.ShapeDtypeStruct((n, out_rows, KV_COLS), BF16)
    return pl.pallas_call(
        functools.partial(_compress_body, rows=rows, tile=tile, out_rows=out_rows),
        grid=(n, rows // tile),
        in_specs=[xspec, xspec, full(pe_row), full(w1bd), full(b1_row), full(w2bd), full(b2_row)],
        out_specs=[ospec, ospec],
        out_shape=[oshape, oshape],
        scratch_shapes=[pltpu.VMEM((2, rows, N_KV * CMP_HIDDEN), F32),
                        pltpu.VMEM((2, rows, N_KV * CMP_HIDDEN), F32)],
        compiler_params=_params("parallel", "arbitrary"),
        name="compress",
    )(xk, xv, pe_row, w1bd, b1_row, w2bd, b2_row)


def _softmax_rows(s, mask):
    s = jnp.where(mask, s, NEG)
    e = jnp.exp(s - jnp.max(s, axis=-1, keepdims=True)) * mask.astype(F32)
    return e / jnp.maximum(jnp.sum(e, axis=-1, keepdims=True), 1e-30)


def _dot_nt(a, b, **kw):
    return lax.dot_general(a, b, (((1,), (1,)), ((), ())), preferred_element_type=F32, **kw)


def _overlap_matrix(n_cmp_rows, n_sel_cols):
    c_start = np.arange(n_cmp_rows)[:, None] * CMP_STRIDE
    s_start = np.arange(n_sel_cols)[None, :] * SEL_BLOCK
    return ((c_start < s_start + SEL_BLOCK) & (c_start + CMP_BLOCK > s_start)).astype(np.float32)


def _nsa_prompt_body(q_ref, kcc_ref, vcc_ref, ks_ref, vs_ref, kw_ref, vw_ref, gates_ref, ovt_ref, o_ref):
    i = pl.program_id(2)
    tq = Q_BLOCK
    rows = GROUP * tq
    n_cmp = kcc_ref.shape[2]
    n_sel = ovt_ref.shape[0]
    q2 = q_ref[0]
    qs = jnp.concatenate([q2[:, r * HEAD_DIM:(r + 1) * HEAD_DIM] for r in range(GROUP)], axis=0)
    tok = lax.broadcasted_iota(jnp.int32, (rows, 1), 0) % tq
    q_pos = i * tq + tok

    s = _dot_nt(qs, kcc_ref[0, 0])
    cmp_end = lax.broadcasted_iota(jnp.int32, (1, n_cmp), 1) * CMP_STRIDE + (CMP_BLOCK - 1)
    p_c = _softmax_rows(s, cmp_end <= q_pos)
    o_c = jnp.dot(p_c.astype(BF16), vcc_ref[0, 0], preferred_element_type=F32)

    p_sum = p_c[0:tq]
    for r in range(1, GROUP):
        p_sum = p_sum + p_c[r * tq:(r + 1) * tq]
    imp = _dot_nt(ovt_ref[...], p_sum, precision=lax.Precision.HIGHEST)
    jj = lax.broadcasted_iota(jnp.int32, (n_sel, tq), 0)
    cur = (i * tq + lax.broadcasted_iota(jnp.int32, (n_sel, tq), 1)) // SEL_BLOCK
    forced = (jj == 0) | (jj == cur) | (jj == cur - 1)
    imp = jnp.where(forced, FORCE, imp)
    imp = jnp.where(jj <= cur, imp, NEG)
    sel = jnp.zeros((n_sel, tq), F32)
    for _ in range(min(N_SEL, n_sel)):
        mx = jnp.max(imp, axis=0, keepdims=True)
        first = jnp.min(jnp.where(imp == mx, jj, n_sel), axis=0, keepdims=True)
        pick = jj == first
        sel = jnp.where(pick, 1.0, sel)
        imp = jnp.where(pick, BELOW_NEG, imp)
    sel_t = sel.T.astype(BF16)

    n_tiles = (i * tq + tq + SEL_TK - 1) // SEL_TK

    def sel_tile(c, carry):
        m, l, acc = carry
        k0 = pl.multiple_of(c * SEL_TK, SEL_TK)
        kt = ks_ref[0, 0, pl.ds(k0, SEL_TK), :]
        vt = vs_ref[0, 0, pl.ds(k0, SEL_TK), :]
        st = _dot_nt(qs, kt)
        key = k0 + lax.broadcasted_iota(jnp.int32, (1, SEL_TK), 1)
        blk_of_key = key // SEL_BLOCK
        expand = (lax.broadcasted_iota(jnp.int32, (n_sel, SEL_TK), 0) == blk_of_key)
        picked = jnp.dot(sel_t, jnp.where(expand, 1.0, 0.0).astype(BF16), preferred_element_type=F32)
        picked = jnp.concatenate([picked] * GROUP, axis=0)
        maskf = jnp.where(key <= q_pos, picked, 0.0)
        st = jnp.where(maskf > 0.0, st, NEG)
        m_new = jnp.maximum(m, jnp.max(st, axis=-1, keepdims=True))
        alpha = jnp.exp(m - m_new)
        p = jnp.exp(st - m_new) * maskf
        l = alpha * l + jnp.sum(p, axis=-1, keepdims=True)
        acc = alpha * acc + jnp.dot(p.astype(BF16), vt, preferred_element_type=F32)
        return m_new, l, acc

    m0 = jnp.full((rows, 1), NEG, F32)
    l0 = jnp.zeros((rows, 1), F32)
    a0 = jnp.zeros((rows, HEAD_DIM), F32)
    _, l_s, acc_s = lax.fori_loop(0, n_tiles, sel_tile, (m0, l0, a0))
    o_s = acc_s / jnp.maximum(l_s, 1e-30)

    wk = WINDOW + tq
    w0 = pl.multiple_of(i * tq, tq)
    kwt = kw_ref[0, 0, pl.ds(w0, wk), :]
    vwt = vw_ref[0, 0, pl.ds(w0, wk), :]
    sw = _dot_nt(qs, kwt)
    kpos = i * tq - WINDOW + lax.broadcasted_iota(jnp.int32, (1, wk), 1)
    dist = q_pos - kpos
    p_w = _softmax_rows(sw, (dist >= 0) & (dist < WINDOW) & (kpos >= 0))
    o_w = jnp.dot(p_w.astype(BF16), vwt, preferred_element_type=F32)

    gt = gates_ref[0]
    gcol = lambda c: jnp.concatenate([gt[:, c * GROUP + r:c * GROUP + r + 1] for r in range(GROUP)], axis=0)
    o = gcol(0) * o_c + gcol(1) * o_s + gcol(2) * o_w
    for r in range(GROUP):
        o_ref[0, :, r * HEAD_DIM:(r + 1) * HEAD_DIM] = o[r * tq:(r + 1) * tq].astype(BF16)


def _nsa_prompt(q, kcc, vcc, ks, vs, kw, vw, gates):
    b, t, _ = q.shape
    n_cmp = kcc.shape[2]
    n_sel = t // SEL_BLOCK
    assert t % SEL_TK == 0 and t % Q_BLOCK == 0
    ovt = jnp.asarray(_overlap_matrix(n_cmp, n_sel).T)
    per_bg = lambda a: pl.BlockSpec((1, 1) + a.shape[2:], lambda bi, g, i: (bi, g, 0, 0))
    return pl.pallas_call(
        _nsa_prompt_body,
        grid=(b, N_KV, t // Q_BLOCK),
        in_specs=[
            pl.BlockSpec((1, Q_BLOCK, GROUP * HEAD_DIM), lambda bi, g, i: (bi, i, g)),
            per_bg(kcc), per_bg(vcc), per_bg(ks), per_bg(vs), per_bg(kw), per_bg(vw),
            pl.BlockSpec((1, Q_BLOCK, LANES), lambda bi, g, i: (bi, i, g)),
            pl.BlockSpec(ovt.shape, lambda bi, g, i: (0, 0)),
        ],
        out_specs=pl.BlockSpec((1, Q_BLOCK, GROUP * HEAD_DIM), lambda bi, g, i: (bi, i, g)),
        out_shape=jax.ShapeDtypeStruct((b, t, N_HEADS * HEAD_DIM), BF16),
        compiler_params=_params("parallel", "parallel", "arbitrary"),
        name="nsa_prompt",
    )(q, kcc, vcc, ks, vs, kw, vw, gates, ovt)


def _to_heads(a):
    b, l, _ = a.shape
    return a.reshape(b, l, N_KV, HEAD_DIM).transpose(0, 2, 1, 3)


def _nsa_prompt_glue(q, kcc, vcc, ks, vs, kw, vw, gates):
    pad = lambda a: jnp.pad(a, ((0, 0), (WINDOW, 0), (0, 0)))
    return _nsa_prompt(q, _to_heads(kcc), _to_heads(vcc), _to_heads(ks), _to_heads(vs),
                       _to_heads(pad(kw)), _to_heads(pad(vw)), gates)


def _nsa_sample_body(q_ref, kcc_ref, vcc_ref, ks_ref, vs_ref, kw_ref, vw_ref, gates_ref, ov_ref, o_ref,
                     *, past_len, n_new, win_buf):
    rows = N_HEADS * n_new
    grows = GROUP * n_new
    n_cmp = kcc_ref.shape[1]
    n_selp = ov_ref.shape[1]
    n_keys = ks_ref.shape[1]
    n_win = kw_ref.shape[1]
    qb = q_ref[0]
    row = lax.broadcasted_iota(jnp.int32, (rows, 1), 0)
    q_pos = past_len + row % n_new

    s = _dot_nt(qb, kcc_ref[0])
    cmp_end = lax.broadcasted_iota(jnp.int32, (1, n_cmp), 1) * CMP_STRIDE + (CMP_BLOCK - 1)
    p_c = _softmax_rows(s, cmp_end <= q_pos)
    o_c = jnp.dot(p_c.astype(BF16), vcc_ref[0], preferred_element_type=F32)

    parts = []
    for g in range(N_KV):
        acc = p_c[g * grows:g * grows + n_new]
        for r in range(1, GROUP):
            acc = acc + p_c[g * grows + r * n_new:g * grows + (r + 1) * n_new]
        parts.append(acc)
    p_sum = jnp.concatenate(parts, axis=0)
    irows = N_KV * n_new
    imp = jnp.dot(p_sum, ov_ref[...], preferred_element_type=F32, precision=lax.Precision.HIGHEST)
    jj = lax.broadcasted_iota(jnp.int32, (irows, n_selp), 1)
    cur = (past_len + lax.broadcasted_iota(jnp.int32, (irows, n_selp), 0) % n_new) // SEL_BLOCK
    forced = (jj == 0) | (jj == cur) | (jj == cur - 1)
    imp = jnp.where(forced, FORCE, imp)
    imp = jnp.where(jj <= cur, imp, NEG)
    sel = jnp.zeros((irows, n_selp), F32)
    n_sel = (past_len + n_new + SEL_BLOCK - 1) // SEL_BLOCK
    for _ in range(min(N_SEL, n_sel)):
        mx = jnp.max(imp, axis=1, keepdims=True)
        first = jnp.min(jnp.where(imp == mx, jj, n_selp), axis=1, keepdims=True)
        pick = jj == first
        sel = jnp.where(pick, 1.0, sel)
        imp = jnp.where(pick, BELOW_NEG, imp)
    sel_b = sel.astype(BF16)

    def sel_tile(c, carry):
        m, l, acc = carry
        k0 = pl.multiple_of(c * SEL_TK, SEL_TK)
        kt = ks_ref[0, pl.ds(k0, SEL_TK), :]
        vt = vs_ref[0, pl.ds(k0, SEL_TK), :]
        st = _dot_nt(qb, kt)
        key = k0 + lax.broadcasted_iota(jnp.int32, (1, SEL_TK), 1)
        expand = lax.broadcasted_iota(jnp.int32, (n_selp, SEL_TK), 0) == key // SEL_BLOCK
        picked = jnp.dot(sel_b, jnp.where(expand, 1.0, 0.0).astype(BF16), preferred_element_type=F32)
        picked = jnp.concatenate(
            [picked[g * n_new:(g + 1) * n_new] for g in range(N_KV) for _ in range(GROUP)], axis=0)
        maskf = jnp.where(key <= q_pos, picked, 0.0)
        st = jnp.where(maskf > 0.0, st, NEG)
        m_new = jnp.maximum(m, jnp.max(st, axis=-1, keepdims=True))
        alpha = jnp.exp(m - m_new)
        p = jnp.exp(st - m_new) * maskf
        l = alpha * l + jnp.sum(p, axis=-1, keepdims=True)
        acc = alpha * acc + jnp.dot(p.astype(BF16), vt, preferred_element_type=F32)
        return m_new, l, acc

    m0 = jnp.full((rows, 1), NEG, F32)
    l0 = jnp.zeros((rows, 1), F32)
    a0 = jnp.zeros((rows, KV_COLS), F32)
    _, l_s, acc_s = lax.fori_loop(0, n_keys // SEL_TK, sel_tile, (m0, l0, a0))
    o_s = acc_s / jnp.maximum(l_s, 1e-30)

    sw = _dot_nt(qb, kw_ref[0])
    kpos = past_len - win_buf + lax.broadcasted_iota(jnp.int32, (1, n_win), 1)
    dist = q_pos - kpos
    p_w = _softmax_rows(sw, (dist >= 0) & (dist < WINDOW) & (kpos >= 0))
    o_w = jnp.dot(p_w.astype(BF16), vw_ref[0], preferred_element_type=F32)

    gt = gates_ref[0]
    o = gt[:, 0:1] * o_c + gt[:, 1:2] * o_s + gt[:, 2:3] * o_w
    diag = lax.broadcasted_iota(jnp.int32, (rows, KV_COLS), 1) // HEAD_DIM == row // grows
    o_ref[0] = jnp.where(diag, o, 0.0)


def _nsa_sample(qbd, kcc, vcc, ks, vs, kw, vw, gate_rows, *, past_len, n_new, win_buf):
    n = qbd.shape[0]
    n_cmp = kcc.shape[1]
    n_selp = 2 * LANES
    assert (past_len + n_new + SEL_BLOCK - 1) // SEL_BLOCK <= n_selp and ks.shape[1] % SEL_TK == 0
    ov = jnp.asarray(_overlap_matrix(n_cmp, n_selp))
    per = lambda a: pl.BlockSpec((1,) + a.shape[1:], lambda i: (i, 0, 0))
    return pl.pallas_call(
        functools.partial(_nsa_sample_body, past_len=past_len, n_new=n_new, win_buf=win_buf),
        grid=(n,),
        in_specs=[per(qbd), per(kcc), per(vcc), per(ks), per(vs), per(kw), per(vw), per(gate_rows),
                  pl.BlockSpec(ov.shape, lambda i: (0, 0))],
        out_specs=pl.BlockSpec((1, N_HEADS * n_new, KV_COLS), lambda i: (i, 0, 0)),
        out_shape=jax.ShapeDtypeStruct((n, N_HEADS * n_new, KV_COLS), F32),
        compiler_params=_params("parallel"),
        name="nsa_sample",
    )(qbd, kcc, vcc, ks, vs, kw, vw, gate_rows, ov)


def _merge_body(x_ref, on_ref, gu_ref, v_ref, sga_ref, sgb_ref, wsg_ref, bsg_ref,
                wpa_ref, wpb_ref, wo_ref, g1_ref, b1_ref, wr_ref, br_ref,
                h_ref, hb_ref, route_ref, *, alpha):
    tm = x_ref.shape[0]
    lane = lax.broadcasted_iota(jnp.int32, (CHUNK, LANES), 1)
    tri = lax.broadcasted_iota(jnp.int32, (CHUNK, CHUNK), 0) >= lax.broadcasted_iota(jnp.int32, (CHUNK, CHUNK), 1)
    zero = jnp.zeros((CHUNK, CHUNK), BF16)
    chunks = []
    for c in range(tm // CHUNK):
        rs = slice(c * CHUNK, (c + 1) * CHUNK)
        cols = []
        for p in range(A_GROUPS // 2):
            cs = slice(p * LANES, (p + 1) * LANES)
            v2 = v_ref[rs, cs].astype(BF16)
            lo = jnp.dot(jnp.where(tri, wsg_ref[0, 2 * p], zero), v2, preferred_element_type=F32)
            hi = jnp.dot(jnp.where(tri, wsg_ref[0, 2 * p + 1], zero), v2, preferred_element_type=F32)
            mixed = jnp.where(lane < A_GDIM, lo, hi) + bsg_ref[0, :, cs]
            cols.append(gu_ref[rs, cs] * mixed)
        chunks.append(jnp.concatenate(cols, axis=1))
    a_out = jnp.concatenate(chunks, axis=0).astype(BF16)
    m = (sga_ref[...] * jnp.dot(a_out, wpa_ref[...], preferred_element_type=F32)
         + sgb_ref[...] * jnp.dot(on_ref[...], wpb_ref[...], preferred_element_type=F32))
    pre = alpha * x_ref[...] + jnp.dot(m.astype(BF16), wo_ref[...], preferred_element_type=F32)
    h = _layer_norm_rows(pre, g1_ref[...], b1_ref[...])
    h_ref[...] = h
    hb_ref[...] = h.astype(BF16)

    lg = jnp.dot(h, wr_ref[...], preferred_element_type=F32, precision=lax.Precision.HIGHEST) + br_ref[...]
    col = lax.broadcasted_iota(jnp.int32, (tm, LANES), 1)
    is_g = col < N_GROUPS
    gl = jnp.where(is_g, lg, BELOW_NEG)
    gmax = jnp.max(gl, axis=1, keepdims=True)
    grp = jnp.min(jnp.where(gl == gmax, col, LANES), axis=1, keepdims=True)
    p_grp = 1.0 / jnp.sum(jnp.where(is_g, jnp.exp(gl - gmax), 0.0), axis=1, keepdims=True)
    e_idx = col - N_GROUPS
    in_grp = (e_idx >= 0) & (e_idx < N_EXPERTS) & (e_idx // EXPERTS_PER_GROUP == grp)
    el = jnp.where(in_grp, lg, BELOW_NEG)
    m1 = jnp.max(el, axis=1, keepdims=True)
    i1 = jnp.min(jnp.where(el == m1, col, LANES), axis=1, keepdims=True)
    el2 = jnp.where(col == i1, BELOW_NEG, el)
    m2 = jnp.max(el2, axis=1, keepdims=True)
    i2 = jnp.min(jnp.where(el2 == m2, col, LANES), axis=1, keepdims=True)
    e2 = jnp.exp(m2 - m1)
    w1 = p_grp / (1.0 + e2)
    w2 = p_grp * e2 / (1.0 + e2)
    route = jnp.where(col == 0, (i1 - N_GROUPS).astype(F32),
                      jnp.where(col == 1, (i2 - N_GROUPS).astype(F32),
                                jnp.where(col == 2, w1, jnp.where(col == 3, w2, 0.0))))
    route_ref[...] = route


def _merge(x, o_nsa, gu, v, sga, sgb, wsg, bsg, wpa, wpb, wo, g1, b1, wr, br, *, alpha, n_first):
    t = x.shape[0]
    tm = MERGE_TM
    assert t % tm == 0
    row = lambda n: pl.BlockSpec((tm, n), lambda i: (i, 0))
    full = lambda a: pl.BlockSpec(a.shape, lambda i: (0,) * a.ndim)
    pick = lambda a: pl.BlockSpec((1,) + a.shape[1:], lambda i: (jnp.where(i < n_first, 0, 1),) + (0,) * (a.ndim - 1))
    return pl.pallas_call(
        functools.partial(_merge_body, alpha=alpha),
        grid=(t // tm,),
        in_specs=[row(D_MODEL), row(D_MODEL), row(A_HALF), row(A_HALF), row(D_MODEL), row(D_MODEL),
                  pick(wsg), pick(bsg), full(wpa), full(wpb), full(wo), full(g1), full(b1), full(wr), full(br)],
        out_specs=[row(D_MODEL), row(D_MODEL), row(LANES)],
        out_shape=[jax.ShapeDtypeStruct((t, D_MODEL), F32), jax.ShapeDtypeStruct((t, D_MODEL), BF16),
                   jax.ShapeDtypeStruct((t, LANES), F32)],
        compiler_params=_params("parallel"),
        name="merge",
    )(x, o_nsa, gu, v, sga, sgb, wsg, bsg, wpa, wpb, wo, g1, b1, wr, br)


def _experts_body(be_ref, xs_ref, sw_ref, wg_ref, wu_ref, wd_ref, o_ref):
    del be_ref
    x = xs_ref[...]
    gate = jnp.dot(x, wg_ref[0], preferred_element_type=F32)
    up = jnp.dot(x, wu_ref[0], preferred_element_type=F32)
    hid = (jax.nn.silu(gate) * up).astype(BF16)
    o_ref[...] = jnp.dot(hid, wd_ref[0], preferred_element_type=F32) * sw_ref[...]


def _experts(blk_exp, xs, slot_w, wg, wu, wd):
    n_slots = xs.shape[0]
    bm = MOE_BM
    grid_spec = pltpu.PrefetchScalarGridSpec(
        num_scalar_prefetch=1,
        grid=(n_slots // bm,),
        in_specs=[
            pl.BlockSpec((bm, D_MODEL), lambda i, be: (i, 0)),
            pl.BlockSpec((bm, 1), lambda i, be: (i, 0)),
            pl.BlockSpec((1, D_MODEL, D_EXPERT), lambda i, be: (be[i], 0, 0)),
            pl.BlockSpec((1, D_MODEL, D_EXPERT), lambda i, be: (be[i], 0, 0)),
            pl.BlockSpec((1, D_EXPERT, D_MODEL), lambda i, be: (be[i], 0, 0)),
        ],
        out_specs=pl.BlockSpec((bm, D_MODEL), lambda i, be: (i, 0)),
    )
    return pl.pallas_call(
        _experts_body,
        grid_spec=grid_spec,
        out_shape=jax.ShapeDtypeStruct((n_slots, D_MODEL), F32),
        compiler_params=_params("arbitrary"),
        name="experts",
    )(blk_exp, xs, slot_w, wg, wu, wd)


def _final_body(h_ref, y_ref, g_ref, b_ref, o_ref, *, alpha):
    o_ref[...] = _layer_norm_rows(alpha * h_ref[...] + y_ref[...], g_ref[...], b_ref[...])


def _final_norm(h, y, g, b, *, alpha):
    t = h.shape[0]
    tm = MERGE_TM
    row = pl.BlockSpec((tm, D_MODEL), lambda i: (i, 0))
    full = lambda a: pl.BlockSpec(a.shape, lambda i: (0,) * a.ndim)
    return pl.pallas_call(
        functools.partial(_final_body, alpha=alpha),
        grid=(t // tm,),
        in_specs=[row, row, full(g), full(b)],
        out_specs=row,
        out_shape=jax.ShapeDtypeStruct((t, D_MODEL), F32),
        compiler_params=_params("parallel"),
        name="final_norm",
    )(h, y, g, b)


def _moe_dispatch(route):
    t = route.shape[0]
    expert = route[:, 0:TOP_K].astype(jnp.int32)
    w_top = route[:, TOP_K:2 * TOP_K]
    n_assign = t * TOP_K
    flat_e = expert.reshape(-1)
    flat_tok = jnp.arange(n_assign, dtype=jnp.int32) // TOP_K
    flat_w = w_top.reshape(-1)
    order = jnp.argsort(flat_e)
    e_s = flat_e[order]
    counts = jnp.zeros((N_EXPERTS,), jnp.int32).at[flat_e].add(1)
    padded = (counts + MOE_BM - 1) // MOE_BM * MOE_BM
    pad_end = jnp.cumsum(padded)
    pad_start = pad_end - padded
    start = jnp.cumsum(counts) - counts
    dest_s = pad_start[e_s] + jnp.arange(n_assign, dtype=jnp.int32) - start[e_s]
    n_blocks = -(-n_assign // MOE_BM) + N_EXPERTS
    n_slots = n_blocks * MOE_BM
    slot_tok = jnp.zeros((n_slots,), jnp.int32).at[dest_s].set(flat_tok[order])
    slot_w = jnp.zeros((n_slots,), F32).at[dest_s].set(flat_w[order])
    dest = jnp.zeros((n_assign,), jnp.int32).at[order].set(dest_s).reshape(t, TOP_K)
    blk_start = jnp.arange(n_blocks, dtype=jnp.int32) * MOE_BM
    blk_exp = jnp.minimum(jnp.sum(pad_end[None, :] <= blk_start[:, None], axis=1), N_EXPERTS - 1).astype(jnp.int32)
    return slot_tok, slot_w, dest, blk_exp


def _sgu_weights(sgu_w, sgu_b, n_new):
    reps = CHUNK // n_new
    w_short = jnp.einsum('ab,gts->gatbs', jnp.eye(reps, dtype=F32), sgu_w[:, :n_new, :n_new]).reshape(A_GROUPS, CHUNK, CHUNK)
    wsg = jnp.stack([sgu_w[:, :CHUNK, :CHUNK], w_short]).astype(BF16)
    b_full = jnp.repeat(sgu_b[:, :CHUNK].T, A_GDIM, axis=1)
    b_short = jnp.tile(jnp.repeat(sgu_b[:, :n_new].T, A_GDIM, axis=1), (reps, 1))
    return wsg, jnp.stack([b_full, b_short])


def kernel(x_prompt, x_sample, cache_cmp_k, cache_cmp_v, cache_sel_k, cache_sel_v, cache_win_k, cache_win_v, page_table, w_in, cmp_pe, cmp_w1, cmp_b1, cmp_w2, cmp_b2, sgu_ln_g, sgu_ln_b, sgu_w, sgu_b, w_proj_a, w_proj_b, w_out, ln1_g, ln1_b, router_group_w, router_group_b, router_expert_w, router_expert_b, exp_w_gate, exp_w_up, exp_w_down, ln2_g, ln2_b):
    depth = w_in.shape[0]
    alpha = (2.0 * depth) ** 0.25
    b, t, _ = x_prompt.shape
    n, n_new, _ = x_sample.shape
    tp, ts = b * t, n * n_new
    past_len = page_table.shape[1] * PAGE_SIZE
    win_buf = cache_win_k.shape[2]
    assert CHUNK % n_new == 0 and tp % MERGE_TM == 0 and ts % MERGE_TM == 0

    x_p, x_s = x_prompt, x_sample
    states = []
    for l in range(depth):
        x_all = jnp.concatenate([x_p.reshape(tp, D_MODEL), x_s.reshape(ts, D_MODEL)], axis=0)
        (q, kc, vc, ks, vs, kw, vw, ksb, vsb, kwb, vwb, gu, v, sga, sgb, gates) = _in_proj(
            x_all, _pack_w_in(w_in[l]), sgu_ln_g[l][None], sgu_ln_b[l][None])
        cw = _compress_weights(cmp_pe[l], cmp_w1[l], cmp_b1[l], cmp_w2[l], cmp_b2[l])

        sub_w = CMP_STRIDE * KV_COLS
        kcc, vcc = _compress(kc[:tp].reshape(b, t // CMP_STRIDE, sub_w), vc[:tp].reshape(b, t // CMP_STRIDE, sub_w),
                             cw, tile=128, out_rows=t // CMP_STRIDE)
        seq = lambda a: a[:tp].reshape(b, t, -1)
        o_p = _nsa_prompt_glue(seq(q), kcc, vcc, seq(ksb), seq(vsb), seq(kwb), seq(vwb), seq(gates))

        new = lambda a: a[tp:].reshape(n, n_new, KV_COLS)
        past = lambda pool: pool[l][page_table].reshape(n, past_len, KV_COLS)
        n_sub = (past_len + n_new + SEL_BLOCK - 1) // SEL_BLOCK * SEL_BLOCK // CMP_STRIDE
        sub_rows = (n_sub + 7) // 8 * 8
        cmp_rows = lambda pool, a: jnp.concatenate(
            [past(pool), new(a), jnp.zeros((n, sub_rows * CMP_STRIDE - past_len - n_new, KV_COLS), F32)],
            axis=1).reshape(n, sub_rows, sub_w)
        cmp_tile = next(c for c in range(128, 7, -8) if sub_rows % c == 0)
        cmp_out = (sub_rows + LANES - 1) // LANES * LANES
        kcc_s, vcc_s = _compress(cmp_rows(cache_cmp_k, kc), cmp_rows(cache_cmp_v, vc), cw, tile=cmp_tile, out_rows=cmp_out)
        n_keys = (past_len + n_new + SEL_TK - 1) // SEL_TK * SEL_TK
        sel_rows = lambda pool, a: jnp.concatenate(
            [past(pool).astype(BF16), new(a), jnp.zeros((n, n_keys - past_len - n_new, KV_COLS), BF16)], axis=1)
        n_win = (win_buf + n_new + LANES - 1) // LANES * LANES
        win_rows = lambda cache, a: jnp.concatenate(
            [cache[l].reshape(n, win_buf, KV_COLS).astype(BF16), new(a),
             jnp.zeros((n, n_win - win_buf - n_new, KV_COLS), BF16)], axis=1)
        q_s = q[tp:].reshape(n, n_new, N_KV, GROUP, HEAD_DIM)
        qbd = jnp.einsum('ntgrd,gj->ngrtjd', q_s, jnp.eye(N_KV, dtype=BF16)).reshape(n, N_HEADS * n_new, KV_COLS)
        g_s = gates[tp:].reshape(n, n_new, N_KV, LANES)[..., :3 * GROUP].reshape(n, n_new, N_KV, 3, GROUP)
        gate_rows = jnp.pad(g_s.transpose(0, 2, 4, 1, 3).reshape(n, N_HEADS * n_new, 3), ((0, 0), (0, 0), (0, LANES - 3)))
        od = _nsa_sample(qbd, kcc_s, vcc_s, sel_rows(cache_sel_k, ksb), sel_rows(cache_sel_v, vsb),
                         win_rows(cache_win_k, kwb), win_rows(cache_win_v, vwb), gate_rows,
                         past_len=past_len, n_new=n_new, win_buf=win_buf)
        o_s = jnp.einsum('ngrtgd->ntgrd', od.reshape(n, N_KV, GROUP, n_new, N_KV, HEAD_DIM)).reshape(ts, N_HEADS * HEAD_DIM)
        o_all = jnp.concatenate([o_p.reshape(tp, -1), o_s.astype(BF16)], axis=0)

        wsg, bsg = _sgu_weights(sgu_w[l], sgu_b[l], n_new)
        wr = jnp.pad(jnp.concatenate([router_group_w[l], router_expert_w[l]], axis=1), ((0, 0), (0, LANES - N_GROUPS - N_EXPERTS)))
        br = jnp.pad(jnp.concatenate([router_group_b[l], router_expert_b[l]]), (0, LANES - N_GROUPS - N_EXPERTS))[None]
        h, hb, route = _merge(x_all, o_all, gu, v, sga, sgb, wsg, bsg,
                              w_proj_a[l].astype(BF16), w_proj_b[l].astype(BF16), w_out[l].astype(BF16),
                              ln1_g[l][None], ln1_b[l][None], wr, br, alpha=alpha, n_first=tp // MERGE_TM)

        slot_tok, slot_w, dest, blk_exp = _moe_dispatch(route)
        out = _experts(blk_exp, hb[slot_tok], slot_w[:, None],
                       exp_w_gate[l].astype(BF16), exp_w_up[l].astype(BF16), exp_w_down[l].astype(BF16))
        y_moe = out[dest[:, 0]] + out[dest[:, 1]]
        y_all = _final_norm(h, y_moe, ln2_g[l][None], ln2_b[l][None], alpha=alpha)

        kv5 = lambda a, lo, hi, bb: a[lo:hi].reshape(bb, -1, N_KV, HEAD_DIM)
        keep_p = min(WINDOW, t)
        prompt_state = tuple(kv5(a, 0, tp, b) for a in (kc, vc, ks, vs)) + tuple(
            kv5(a, 0, tp, b)[:, t - keep_p:] for a in (kw, vw))
        keep_s = min(WINDOW, win_buf + n_new)
        win_all = lambda cache, a: jnp.concatenate([cache[l], kv5(a, tp, tp + ts, n)], axis=1)[:, win_buf + n_new - keep_s:]
        sample_state = tuple(kv5(a, tp, tp + ts, n) for a in (kc, vc, ks, vs)) + (
            win_all(cache_win_k, kw), win_all(cache_win_v, vw), v[tp:].reshape(n, n_new, A_HALF))
        states.append(prompt_state + sample_state)
        x_p, x_s = y_all[:tp].reshape(b, t, D_MODEL), y_all[tp:].reshape(n, n_new, D_MODEL)
    return (x_p, x_s) + tuple(jnp.stack(z) for z in zip(*states))
```

```python
import functools

import numpy as np
import jax
import jax.numpy as jnp
from jax import lax
from jax.experimental import pallas as pl
from jax.experimental.pallas import tpu as pltpu

F32 = jnp.float32
BF16 = jnp.bfloat16

D_MODEL = 1024
N_HEADS = 16
HEAD_DIM = 64
N_KV = 4
GROUP = N_HEADS // N_KV
KV_COLS = N_KV * HEAD_DIM
CMP_STRIDE = 16
CMP_BLOCK = 32
CMP_HIDDEN = 64
SEL_BLOCK = 64
N_SEL = 16
WINDOW = 512
Q_BLOCK = 128
PAGE_SIZE = 128
CHUNK = 128
A_GROUPS = 8
A_HALF = D_MODEL // 2
A_GDIM = A_HALF // A_GROUPS
N_GROUPS = 4
EXPERTS_PER_GROUP = 8
N_EXPERTS = N_GROUPS * EXPERTS_PER_GROUP
TOP_K = 2
D_EXPERT = D_MODEL // 2
LN_EPS = 1e-5
NEG = -1e30
FORCE = 1e9
BELOW_NEG = -3e38

LANES = 128
VMEM_LIMIT = 56 * 1024 * 1024

INPROJ_TM = 256
MERGE_TM = 256
MOE_BM = 256
SEL_TK = 512


def _gelu(x):
    return jax.nn.gelu(x, approximate=True)


def _layer_norm_rows(x, g, b):
    mu = jnp.mean(x, axis=-1, keepdims=True)
    xc = x - mu
    var = jnp.mean(xc * xc, axis=-1, keepdims=True)
    return xc * lax.rsqrt(var + LN_EPS) * g + b


def _params(*sem):
    return pltpu.CompilerParams(dimension_semantics=tuple(sem), vmem_limit_bytes=VMEM_LIMIT)


_C_Q = 0
_C_KV = D_MODEL
_C_AU = _C_KV + 6 * KV_COLS
_C_AV = _C_AU + A_HALF
_C_GA = _C_AV + A_HALF
_C_GB = _C_GA + D_MODEL
_C_GATE = _C_GB + D_MODEL
_C_END = _C_GATE + N_KV * LANES


def _pack_w_in(w_in):
    q_cols = N_HEADS * HEAD_DIM
    o_gate = q_cols + 6 * KV_COLS
    o_au = o_gate + 3 * N_HEADS
    src = np.zeros((N_KV * LANES,), np.int32)
    valid = np.zeros((N_KV * LANES,), np.float32)
    for g in range(N_KV):
        for c in range(3):
            for r in range(GROUP):
                src[g * LANES + c * GROUP + r] = o_gate + (g * GROUP + r) * 3 + c
                valid[g * LANES + c * GROUP + r] = 1.0
    w_gate = w_in[:, src] * jnp.asarray(valid)
    packed = jnp.concatenate([w_in[:, :o_gate], w_in[:, o_au:], w_gate], axis=1)
    return packed.astype(BF16)


def _inproj_body(x_ref, w_ref, lng_ref, lnb_ref,
                 q_ref, kc_ref, vc_ref, ks_ref, vs_ref, kw_ref, vw_ref,
                 ksb_ref, vsb_ref, kwb_ref, vwb_ref,
                 gu_ref, v_ref, sga_ref, sgb_ref, gates_ref):
    x = x_ref[...].astype(BF16)

    def mm(c0, n):
        return jnp.dot(x, w_ref[:, c0:c0 + n], preferred_element_type=F32)

    q_ref[...] = (mm(_C_Q, D_MODEL) * (HEAD_DIM ** -0.5)).astype(BF16)
    f32_refs = (kc_ref, vc_ref, ks_ref, vs_ref, kw_ref, vw_ref)
    bf_refs = (None, None, ksb_ref, vsb_ref, kwb_ref, vwb_ref)
    for i in range(6):
        y = mm(_C_KV + i * KV_COLS, KV_COLS)
        f32_refs[i][...] = y
        if bf_refs[i] is not None:
            bf_refs[i][...] = y.astype(BF16)
    gu_ref[...] = _gelu(mm(_C_AU, A_HALF))
    v_ref[...] = _layer_norm_rows(_gelu(mm(_C_AV, A_HALF)), lng_ref[...], lnb_ref[...])
    sga_ref[...] = jax.nn.sigmoid(mm(_C_GA, D_MODEL))
    sgb_ref[...] = jax.nn.sigmoid(mm(_C_GB, D_MODEL))
    gates_ref[...] = jax.nn.sigmoid(mm(_C_GATE, N_KV * LANES))


def _in_proj(x, w_packed, ln_g, ln_b):
    t = x.shape[0]
    tm = INPROJ_TM
    assert t % tm == 0
    row = lambda n: pl.BlockSpec((tm, n), lambda i: (i, 0))
    full = lambda a: pl.BlockSpec(a.shape, lambda i: (0,) * a.ndim)
    out_shapes = (
        [jax.ShapeDtypeStruct((t, D_MODEL), BF16)]
        + [jax.ShapeDtypeStruct((t, KV_COLS), F32)] * 6
        + [jax.ShapeDtypeStruct((t, KV_COLS), BF16)] * 4
        + [jax.ShapeDtypeStruct((t, A_HALF), F32)] * 2
        + [jax.ShapeDtypeStruct((t, D_MODEL), F32)] * 2
        + [jax.ShapeDtypeStruct((t, N_KV * LANES), F32)]
    )
    out_specs = [row(s.shape[1]) for s in out_shapes]
    return pl.pallas_call(
        _inproj_body,
        grid=(t // tm,),
        in_specs=[row(D_MODEL), full(w_packed), full(ln_g), full(ln_b)],
        out_specs=out_specs,
        out_shape=out_shapes,
        compiler_params=_params("parallel"),
        name="in_proj",
    )(x, w_packed, ln_g, ln_b)


def _compress_weights(pe, w1, b1, w2, b2):
    eye = jnp.eye(N_KV, dtype=F32)
    w1r = w1.reshape(2, 2, CMP_STRIDE, HEAD_DIM, CMP_HIDDEN)
    w1bd = jnp.einsum('kfsdh,gj->kfsgdjh', w1r, eye).reshape(
        2, 2, CMP_STRIDE * KV_COLS, N_KV * CMP_HIDDEN).astype(BF16)
    per = pe.reshape(2, 2, CMP_STRIDE, 1, HEAD_DIM)
    pe_row = jnp.broadcast_to(per, (2, 2, CMP_STRIDE, N_KV, HEAD_DIM)).reshape(2, 2, 1, CMP_STRIDE * KV_COLS)
    b1_row = jnp.tile(b1, (1, N_KV)).reshape(2, 1, N_KV * CMP_HIDDEN)
    w2bd = jnp.einsum('khd,gj->kghjd', w2, eye).reshape(2, N_KV * CMP_HIDDEN, KV_COLS).astype(BF16)
    b2_row = jnp.tile(b2, (1, N_KV)).reshape(2, 1, KV_COLS)
    return pe_row, w1bd, b1_row, w2bd, b2_row


def _compress_body(xk_ref, xv_ref, pe_ref, w1_ref, b1_ref, w2_ref, b2_ref, ok_ref, ov_ref,
                   first_ref, second_ref, *, rows, tile, out_rows):
    j = pl.program_id(1)
    r0 = pl.multiple_of(j * tile, 8)
    for kv, x_ref in enumerate((xk_ref, xv_ref)):
        x = x_ref[0]
        first_ref[kv, pl.ds(r0, tile), :] = jnp.dot(
            (x + pe_ref[kv, 0]).astype(BF16), w1_ref[kv, 0], preferred_element_type=F32)
        second_ref[kv, pl.ds(r0, tile), :] = jnp.dot(
            (x + pe_ref[kv, 1]).astype(BF16), w1_ref[kv, 1], preferred_element_type=F32)

    @pl.when(j == pl.num_programs(1) - 1)
    def _():
        for kv, o_ref in enumerate((ok_ref, ov_ref)):
            nxt = pltpu.roll(second_ref[kv], rows - 1, axis=0)
            hid = _gelu(first_ref[kv] + nxt + b1_ref[kv])
            out = jnp.dot(hid.astype(BF16), w2_ref[kv], preferred_element_type=F32) + b2_ref[kv]
            o_ref[0, 0:rows, :] = out.astype(BF16)
            if out_rows > rows:
                o_ref[0, rows:out_rows, :] = jnp.zeros((out_rows - rows, KV_COLS), BF16)


def _compress(xk, xv, cw, *, tile, out_rows):
    pe_row, w1bd, b1_row, w2bd, b2_row = cw
    n, rows, width = xk.shape
    assert rows % tile == 0 and tile % 8 == 0
    xspec = pl.BlockSpec((1, tile, width), lambda i, j: (i, j, 0))
    full = lambda a: pl.BlockSpec(a.shape, lambda i, j: (0,) * a.ndim)
    ospec = pl.BlockSpec((1, out_rows, KV_COLS), lambda i, j: (i, 0, 0))
    oshape = jax.ShapeDtypeStruct((n, out_rows, KV_COLS), BF16)
    return pl.pallas_call(
        functools.partial(_compress_body, rows=rows, tile=tile, out_rows=out_rows),
        grid=(n, rows // tile),
        in_specs=[xspec, xspec, full(pe_row), full(w1bd), full(b1_row), full(w2bd), full(b2_row)],
        out_specs=[ospec, ospec],
        out_shape=[oshape, oshape],
        scratch_shapes=[pltpu.VMEM((2, rows, N_KV * CMP_HIDDEN), F32),
                        pltpu.VMEM((2, rows, N_KV * CMP_HIDDEN), F32)],
        compiler_params=_params("parallel", "arbitrary"),
        name="compress",
    )(xk, xv, pe_row, w1bd, b1_row, w2bd, b2_row)


def _softmax_rows(s, mask):
    s = jnp.where(mask, s, NEG)
    e = jnp.exp(s - jnp.max(s, axis=-1, keepdims=True)) * mask.astype(F32)
    return e / jnp.maximum(jnp.sum(e, axis=-1, keepdims=True), 1e-30)


def _dot_nt(a, b, **kw):
    return lax.dot_general(a, b, (((1,), (1,)), ((), ())), preferred_element_type=F32, **kw)


def _overlap_matrix(n_cmp_rows, n_sel_cols):
    c_start = np.arange(n_cmp_rows)[:, None] * CMP_STRIDE
    s_start = np.arange(n_sel_cols)[None, :] * SEL_BLOCK
    return ((c_start < s_start + SEL_BLOCK) & (c_start + CMP_BLOCK > s_start)).astype(np.float32)


def _nsa_prompt_body(q_ref, kcc_ref, vcc_ref, ks_ref, vs_ref, kw_ref, vw_ref, gates_ref, ovt_ref, wb_ref, o_ref):
    i = pl.program_id(2)
    tq = Q_BLOCK
    rows = GROUP * tq
    n_cmp = kcc_ref.shape[2]
    n_sel = ovt_ref.shape[0]
    q2 = q_ref[0]
    qs = jnp.concatenate([q2[:, r * HEAD_DIM:(r + 1) * HEAD_DIM] for r in range(GROUP)], axis=0)
    tok = lax.broadcasted_iota(jnp.int32, (rows, 1), 0) % tq
    q_pos = i * tq + tok

    cmp_end = lax.broadcasted_iota(jnp.int32, (1, n_cmp), 1) * CMP_STRIDE + (CMP_BLOCK - 1)
    s = jnp.where(cmp_end <= q_pos, _dot_nt(qs, kcc_ref[0, 0]), NEG)
    e_c = jnp.exp(s - jnp.max(s, axis=-1, keepdims=True))
    p_c = e_c * jnp.where(q_pos >= CMP_BLOCK - 1, 1.0 / jnp.sum(e_c, axis=-1, keepdims=True), 0.0)
    o_c = jnp.dot(p_c.astype(BF16), vcc_ref[0, 0], preferred_element_type=F32)

    p_sum = p_c[0:tq]
    for r in range(1, GROUP):
        p_sum = p_sum + p_c[r * tq:(r + 1) * tq]
    imp = _dot_nt(ovt_ref[...], p_sum, precision=lax.Precision.HIGHEST)
    jj = lax.broadcasted_iota(jnp.int32, (n_sel, tq), 0)
    cur = (i * tq + lax.broadcasted_iota(jnp.int32, (n_sel, tq), 1)) // SEL_BLOCK
    forced = (jj == 0) | (jj == cur) | (jj == cur - 1)
    imp = jnp.where(forced, FORCE, imp)
    imp = jnp.where(jj <= cur, imp, NEG)
    sel = jnp.zeros((n_sel, tq), F32)
    for _ in range(min(N_SEL, n_sel)):
        mx = jnp.max(imp, axis=0, keepdims=True)
        first = jnp.min(jnp.where(imp == mx, jj, n_sel), axis=0, keepdims=True)
        pick = jj == first
        sel = jnp.where(pick, 1.0, sel)
        imp = jnp.where(pick, BELOW_NEG, imp)
    nb = jnp.where(sel.T > 0.0, 0.0, NEG).astype(BF16)
    qa = jnp.concatenate([jnp.concatenate([nb] * GROUP, axis=0), qs], axis=1)

    def sel_tile(c, carry, causal):
        m, l, acc = carry
        k0 = pl.multiple_of(c * SEL_TK, SEL_TK)
        st = _dot_nt(qa, ks_ref[0, 0, pl.ds(k0, SEL_TK), :])
        if causal:
            key = k0 + lax.broadcasted_iota(jnp.int32, (1, SEL_TK), 1)
            st = jnp.where(key <= q_pos, st, NEG)
        m_new = jnp.maximum(m, jnp.max(st, axis=-1, keepdims=True))
        alpha = jnp.exp(m - m_new)
        p = jnp.exp(st - m_new)
        l = alpha * l + jnp.sum(p, axis=-1, keepdims=True)
        acc = alpha * acc + jnp.dot(p.astype(BF16), vs_ref[0, 0, pl.ds(k0, SEL_TK), :], preferred_element_type=F32)
        return m_new, l, acc

    carry = (jnp.full((rows, 1), NEG, F32), jnp.zeros((rows, 1), F32), jnp.zeros((rows, HEAD_DIM), F32))
    n_full = (i * tq) // SEL_TK
    carry = lax.fori_loop(0, n_full, functools.partial(sel_tile, causal=False), carry)
    _, l_s, acc_s = sel_tile(n_full, carry, True)
    o_s = acc_s / l_s

    wk = WINDOW + tq
    w0 = pl.multiple_of(i * tq, tq)
    kpos = i * tq - WINDOW + lax.broadcasted_iota(jnp.int32, (1, wk), 1)
    sw = _dot_nt(qs, kw_ref[0, 0, pl.ds(w0, wk), :]) + wb_ref[...] + jnp.where(kpos >= 0, 0.0, NEG)
    e_w = jnp.exp(sw - jnp.max(sw, axis=-1, keepdims=True))
    p_w = e_w * (1.0 / jnp.sum(e_w, axis=-1, keepdims=True))
    o_w = jnp.dot(p_w.astype(BF16), vw_ref[0, 0, pl.ds(w0, wk), :], preferred_element_type=F32)

    gt = gates_ref[0]
    gcol = lambda c: jnp.concatenate([gt[:, c * GROUP + r:c * GROUP + r + 1] for r in range(GROUP)], axis=0)
    o = gcol(0) * o_c + gcol(1) * o_s + gcol(2) * o_w
    for r in range(GROUP):
        o_ref[0, :, r * HEAD_DIM:(r + 1) * HEAD_DIM] = o[r * tq:(r + 1) * tq].astype(BF16)


def _nsa_prompt(q, kcc, vcc, ks, vs, kw, vw, gates):
    b, t, _ = q.shape
    n_cmp = kcc.shape[2]
    n_sel = t // SEL_BLOCK
    assert t % SEL_TK == 0 and t % Q_BLOCK == 0 and ks.shape[3] == n_sel + HEAD_DIM
    ovt = jnp.asarray(_overlap_matrix(n_cmp, n_sel).T)
    tt = np.arange(GROUP * Q_BLOCK)[:, None] % Q_BLOCK
    col = np.arange(WINDOW + Q_BLOCK)[None, :]
    wb = jnp.asarray(np.where((col > tt) & (col <= tt + WINDOW), 0.0, NEG).astype(np.float32))
    per_bg = lambda a: pl.BlockSpec((1, 1) + a.shape[2:], lambda bi, g, i: (bi, g, 0, 0))
    return pl.pallas_call(
        _nsa_prompt_body,
        grid=(b, N_KV, t // Q_BLOCK),
        in_specs=[
            pl.BlockSpec((1, Q_BLOCK, GROUP * HEAD_DIM), lambda bi, g, i: (bi, i, g)),
            per_bg(kcc), per_bg(vcc), per_bg(ks), per_bg(vs), per_bg(kw), per_bg(vw),
            pl.BlockSpec((1, Q_BLOCK, LANES), lambda bi, g, i: (bi, i, g)),
            pl.BlockSpec(ovt.shape, lambda bi, g, i: (0, 0)),
            pl.BlockSpec(wb.shape, lambda bi, g, i: (0, 0)),
        ],
        out_specs=pl.BlockSpec((1, Q_BLOCK, GROUP * HEAD_DIM), lambda bi, g, i: (bi, i, g)),
        out_shape=jax.ShapeDtypeStruct((b, t, N_HEADS * HEAD_DIM), BF16),
        compiler_params=_params("parallel", "parallel", "arbitrary"),
        name="nsa_prompt",
    )(q, kcc, vcc, ks, vs, kw, vw, gates, ovt, wb)


def _to_heads(a):
    b, l, _ = a.shape
    return a.reshape(b, l, N_KV, HEAD_DIM).transpose(0, 2, 1, 3)


def _nsa_prompt_glue(q, kcc, vcc, ks, vs, kw, vw, gates):
    pad = lambda a: jnp.pad(a, ((0, 0), (WINDOW, 0), (0, 0)))
    b, t, _ = ks.shape
    n_sel = t // SEL_BLOCK
    onehot = (jnp.arange(t)[:, None] // SEL_BLOCK == jnp.arange(n_sel)[None, :]).astype(BF16)
    ks_aug = jnp.concatenate([jnp.broadcast_to(onehot, (b, N_KV, t, n_sel)), _to_heads(ks)], axis=3)
    return _nsa_prompt(q, _to_heads(kcc), _to_heads(vcc), ks_aug, _to_heads(vs),
                       _to_heads(pad(kw)), _to_heads(pad(vw)), gates)


def _nsa_sample_body(q_ref, kcc_ref, vcc_ref, ks_ref, vs_ref, kw_ref, vw_ref, gates_ref, ov_ref, o_ref,
                     *, past_len, n_new, win_buf):
    rows = N_HEADS * n_new
    grows = GROUP * n_new
    n_cmp = kcc_ref.shape[1]
    n_selp = ov_ref.shape[1]
    n_keys = ks_ref.shape[1]
    n_win = kw_ref.shape[1]
    qb = q_ref[0]
    row = lax.broadcasted_iota(jnp.int32, (rows, 1), 0)
    q_pos = past_len + row % n_new

    s = _dot_nt(qb, kcc_ref[0])
    cmp_end = lax.broadcasted_iota(jnp.int32, (1, n_cmp), 1) * CMP_STRIDE + (CMP_BLOCK - 1)
    p_c = _softmax_rows(s, cmp_end <= q_pos)
    o_c = jnp.dot(p_c.astype(BF16), vcc_ref[0], preferred_element_type=F32)

    parts = []
    for g in range(N_KV):
        acc = p_c[g * grows:g * grows + n_new]
        for r in range(1, GROUP):
            acc = acc + p_c[g * grows + r * n_new:g * grows + (r + 1) * n_new]
        parts.append(acc)
    p_sum = jnp.concatenate(parts, axis=0)
    irows = N_KV * n_new
    imp = jnp.dot(p_sum, ov_ref[...], preferred_element_type=F32, precision=lax.Precision.HIGHEST)
    jj = lax.broadcasted_iota(jnp.int32, (irows, n_selp), 1)
    cur = (past_len + lax.broadcasted_iota(jnp.int32, (irows, n_selp), 0) % n_new) // SEL_BLOCK
    forced = (jj == 0) | (jj == cur) | (jj == cur - 1)
    imp = jnp.where(forced, FORCE, imp)
    imp = jnp.where(jj <= cur, imp, NEG)
    sel = jnp.zeros((irows, n_selp), F32)
    n_sel = (past_len + n_new + SEL_BLOCK - 1) // SEL_BLOCK
    for _ in range(min(N_SEL, n_sel)):
        mx = jnp.max(imp, axis=1, keepdims=True)
        first = jnp.min(jnp.where(imp == mx, jj, n_selp), axis=1, keepdims=True)
        pick = jj == first
        sel = jnp.where(pick, 1.0, sel)
        imp = jnp.where(pick, BELOW_NEG, imp)
    sel_b = sel.astype(BF16)

    def sel_tile(c, carry):
        m, l, acc = carry
        k0 = pl.multiple_of(c * SEL_TK, SEL_TK)
        kt = ks_ref[0, pl.ds(k0, SEL_TK), :]
        vt = vs_ref[0, pl.ds(k0, SEL_TK), :]
        st = _dot_nt(qb, kt)
        key = k0 + lax.broadcasted_iota(jnp.int32, (1, SEL_TK), 1)
        expand = lax.broadcasted_iota(jnp.int32, (n_selp, SEL_TK), 0) == key // SEL_BLOCK
        picked = jnp.dot(sel_b, jnp.where(expand, 1.0, 0.0).astype(BF16), preferred_element_type=F32)
        picked = jnp.concatenate(
            [picked[g * n_new:(g + 1) * n_new] for g in range(N_KV) for _ in range(GROUP)], axis=0)
        maskf = jnp.where(key <= q_pos, picked, 0.0)
        st = jnp.where(maskf > 0.0, st, NEG)
        m_new = jnp.maximum(m, jnp.max(st, axis=-1, keepdims=True))
        alpha = jnp.exp(m - m_new)
        p = jnp.exp(st - m_new) * maskf
        l = alpha * l + jnp.sum(p, axis=-1, keepdims=True)
        acc = alpha * acc + jnp.dot(p.astype(BF16), vt, preferred_element_type=F32)
        return m_new, l, acc

    m0 = jnp.full((rows, 1), NEG, F32)
    l0 = jnp.zeros((rows, 1), F32)
    a0 = jnp.zeros((rows, KV_COLS), F32)
    _, l_s, acc_s = lax.fori_loop(0, n_keys // SEL_TK, sel_tile, (m0, l0, a0))
    o_s = acc_s / jnp.maximum(l_s, 1e-30)

    sw = _dot_nt(qb, kw_ref[0])
    kpos = past_len - win_buf + lax.broadcasted_iota(jnp.int32, (1, n_win), 1)
    dist = q_pos - kpos
    p_w = _softmax_rows(sw, (dist >= 0) & (dist < WINDOW) & (kpos >= 0))
    o_w = jnp.dot(p_w.astype(BF16), vw_ref[0], preferred_element_type=F32)

    gt = gates_ref[0]
    o = gt[:, 0:1] * o_c + gt[:, 1:2] * o_s + gt[:, 2:3] * o_w
    diag = lax.broadcasted_iota(jnp.int32, (rows, KV_COLS), 1) // HEAD_DIM == row // grows
    o_ref[0] = jnp.where(diag, o, 0.0)


def _nsa_sample(qbd, kcc, vcc, ks, vs, kw, vw, gate_rows, *, past_len, n_new, win_buf):
    n = qbd.shape[0]
    n_cmp = kcc.shape[1]
    n_selp = 2 * LANES
    assert (past_len + n_new + SEL_BLOCK - 1) // SEL_BLOCK <= n_selp and ks.shape[1] % SEL_TK == 0
    ov = jnp.asarray(_overlap_matrix(n_cmp, n_selp))
    per = lambda a: pl.BlockSpec((1,) + a.shape[1:], lambda i: (i, 0, 0))
    return pl.pallas_call(
        functools.partial(_nsa_sample_body, past_len=past_len, n_new=n_new, win_buf=win_buf),
        grid=(n,),
        in_specs=[per(qbd), per(kcc), per(vcc), per(ks), per(vs), per(kw), per(vw), per(gate_rows),
                  pl.BlockSpec(ov.shape, lambda i: (0, 0))],
        out_specs=pl.BlockSpec((1, N_HEADS * n_new, KV_COLS), lambda i: (i, 0, 0)),
        out_shape=jax.ShapeDtypeStruct((n, N_HEADS * n_new, KV_COLS), F32),
        compiler_params=_params("parallel"),
        name="nsa_sample",
    )(qbd, kcc, vcc, ks, vs, kw, vw, gate_rows, ov)


def _page_view(pool):
    return jnp.transpose(pool, (0, 2, 3, 1)).reshape(pool.shape[0], KV_COLS, pool.shape[1])


def _page_copy(pool_ref, page, buf_ref, slot, j, sem_ref):
    return pltpu.make_async_copy(pool_ref.at[page], buf_ref.at[slot, j], sem_ref.at[slot])


def _start_pages(pt_ref, seq, pool_ref, buf_ref, slot, sem_ref, n_pages):
    def body(j, c):
        _page_copy(pool_ref, pt_ref[seq, j], buf_ref, slot, j, sem_ref).start()
        return c
    lax.fori_loop(0, n_pages, body, 0)


def _wait_pages(pool_ref, buf_ref, slot, sem_ref, n_pages):
    def body(j, c):
        _page_copy(pool_ref, 0, buf_ref, slot, j, sem_ref).wait()
        return c
    lax.fori_loop(0, n_pages, body, 0)


def _compress_paged_body(pt_ref, pool_k, pool_v, new_ref, perm_ref, w1_ref, b1_ref, w2_ref, b2_ref,
                         o_ref, pbuf, xsub, sem, *, n_pages, out_rows):
    kv = pl.program_id(0)
    i = pl.program_id(1)
    n_seq = pl.num_programs(1)
    step = kv * n_seq + i
    slot = step % 2

    def start(kv_idx, seq, slot_idx):
        @pl.when(kv_idx == 0)
        def _():
            _start_pages(pt_ref, seq, pool_k, pbuf, slot_idx, sem, n_pages)

        @pl.when(kv_idx == 1)
        def _():
            _start_pages(pt_ref, seq, pool_v, pbuf, slot_idx, sem, n_pages)

    @pl.when(step == 0)
    def _():
        start(kv, i, slot)

    @pl.when(i + 1 < n_seq)
    def _():
        start(kv, i + 1, 1 - slot)

    @pl.when((i + 1 == n_seq) & (kv == 0))
    def _():
        start(kv + 1, 0, 1 - slot)

    _wait_pages(pool_k, pbuf, slot, sem, n_pages)

    def pair(p, c):
        xt = jnp.concatenate([pbuf[slot, 2 * p].astype(BF16), pbuf[slot, 2 * p + 1].astype(BF16)], axis=1)
        y = _dot_nt(perm_ref[...], xt)
        r0 = pl.multiple_of(p * 16, 16)
        for s in range(CMP_STRIDE):
            xsub[pl.ds(r0, 16), s * KV_COLS:(s + 1) * KV_COLS] = y[s * 16:(s + 1) * 16].astype(BF16)
        return c
    lax.fori_loop(0, n_pages // 2, pair, 0)
    past_sub = n_pages * (PAGE_SIZE // CMP_STRIDE)
    rows = past_sub + 16
    xsub[past_sub:rows, :] = new_ref[0, 0]

    y = jnp.dot(xsub[...], w1_ref[0], preferred_element_type=F32)
    h4 = N_KV * CMP_HIDDEN
    pe_first = y[past_sub + 4:past_sub + 5, :h4] + y[past_sub + 5:past_sub + 6, :h4]
    pe_second = y[past_sub + 6:past_sub + 7, h4:] + y[past_sub + 7:past_sub + 8, h4:]
    first = y[:, :h4] + pe_first
    nxt = pltpu.roll(y[:, h4:] + pe_second, rows - 1, axis=0)
    hid = _gelu(first + nxt + b1_ref[0])
    out = jnp.dot(hid.astype(BF16), w2_ref[0], preferred_element_type=F32) + b2_ref[0]
    o_ref[0, 0, 0:rows, :] = out.astype(BF16)
    o_ref[0, 0, rows:out_rows, :] = jnp.zeros((out_rows - rows, KV_COLS), BF16)


def _compress_paged(page_table, pool_k, pool_v, new_kv, cw, *, out_rows):
    pe_row, w1bd, b1_row, w2bd, b2_row = cw
    n, n_pages = page_table.shape
    n_new = new_kv.shape[2]
    assert n_pages % 2 == 0 and n_new <= CMP_STRIDE
    past_sub = n_pages * (PAGE_SIZE // CMP_STRIDE)
    rows = past_sub + 16
    sub_w = CMP_STRIDE * KV_COLS
    pe = pe_row.reshape(2, 2, sub_w)
    pe_hi = pe.astype(BF16)
    pe_lo = (pe - pe_hi.astype(F32)).astype(BF16)
    pe_rows = jnp.stack([pe_hi[:, 0], pe_lo[:, 0], pe_hi[:, 1], pe_lo[:, 1]], axis=1)
    new_row = jnp.pad(new_kv.reshape(2, n, 1, n_new * KV_COLS), ((0, 0), (0, 0), (0, 3), (0, sub_w - n_new * KV_COLS)))
    new_sub = jnp.concatenate(
        [new_row.astype(BF16), jnp.broadcast_to(pe_rows[:, None], (2, n, 4, sub_w)),
         jnp.zeros((2, n, 8, sub_w), BF16)], axis=2)
    w1cat = jnp.concatenate([w1bd[:, 0], w1bd[:, 1]], axis=2)
    m = np.arange(256)
    perm = np.zeros((256, 256), np.float32)
    perm[m, (m % 16) * 16 + m // 16] = 1.0
    perm = jnp.asarray(perm, BF16)
    grid_spec = pltpu.PrefetchScalarGridSpec(
        num_scalar_prefetch=1,
        grid=(2, n),
        in_specs=[
            pl.BlockSpec(memory_space=pl.ANY), pl.BlockSpec(memory_space=pl.ANY),
            pl.BlockSpec((1, 1, 16, sub_w), lambda kv, i, pt: (kv, i, 0, 0)),
            pl.BlockSpec(perm.shape, lambda kv, i, pt: (0, 0)),
            pl.BlockSpec((1,) + w1cat.shape[1:], lambda kv, i, pt: (kv, 0, 0)),
            pl.BlockSpec((1,) + b1_row.shape[1:], lambda kv, i, pt: (kv, 0, 0)),
            pl.BlockSpec((1,) + w2bd.shape[1:], lambda kv, i, pt: (kv, 0, 0)),
            pl.BlockSpec((1,) + b2_row.shape[1:], lambda kv, i, pt: (kv, 0, 0)),
        ],
        out_specs=pl.BlockSpec((1, 1, out_rows, KV_COLS), lambda kv, i, pt: (kv, i, 0, 0)),
        scratch_shapes=[pltpu.VMEM((2, n_pages, KV_COLS, PAGE_SIZE), F32),
                        pltpu.VMEM((rows, sub_w), BF16),
                        pltpu.SemaphoreType.DMA((2,))],
    )
    return pl.pallas_call(
        functools.partial(_compress_paged_body, n_pages=n_pages, out_rows=out_rows),
        grid_spec=grid_spec,
        out_shape=jax.ShapeDtypeStruct((2, n, out_rows, KV_COLS), BF16),
        compiler_params=_params("arbitrary", "arbitrary"),
        name="compress_paged",
    )(page_table, pool_k, pool_v, new_sub, perm, w1cat, b1_row, w2bd, b2_row)


PAGES_PER_TILE = SEL_TK // PAGE_SIZE


def _nsa_sample_paged_body(pt_ref, pool_k, pool_v, q_ref, kcc_ref, vcc_ref, ksn_ref, vsn_ref,
                           kwc_ref, vwc_ref, kwn_ref, vwn_ref, gates_ref, ov_ref, o_ref,
                           kbuf, vbuf, ksem, vsem, *, n_pages, n_new, win_buf):
    i = pl.program_id(0)
    n_seq = pl.num_programs(0)
    slot = i % 2
    past_len = n_pages * PAGE_SIZE

    def start(seq, slot_idx):
        _start_pages(pt_ref, seq, pool_k, kbuf, slot_idx, ksem, n_pages)
        _start_pages(pt_ref, seq, pool_v, vbuf, slot_idx, vsem, n_pages)

    @pl.when(i == 0)
    def _():
        start(i, slot)

    @pl.when(i + 1 < n_seq)
    def _():
        start(i + 1, 1 - slot)

    rows = N_HEADS * n_new
    grows = GROUP * n_new
    n_cmp = kcc_ref.shape[2]
    n_selp = ov_ref.shape[1]
    qb = q_ref[0]
    row = lax.broadcasted_iota(jnp.int32, (rows, 1), 0)
    q_pos = past_len + row % n_new

    s = _dot_nt(qb, kcc_ref[0, 0])
    cmp_end = lax.broadcasted_iota(jnp.int32, (1, n_cmp), 1) * CMP_STRIDE + (CMP_BLOCK - 1)
    p_c = _softmax_rows(s, cmp_end <= q_pos)
    o_c = jnp.dot(p_c.astype(BF16), vcc_ref[0, 0], preferred_element_type=F32)

    parts = []
    for g in range(N_KV):
        acc = p_c[g * grows:g * grows + n_new]
        for r in range(1, GROUP):
            acc = acc + p_c[g * grows + r * n_new:g * grows + (r + 1) * n_new]
        parts.append(acc)
    p_sum = jnp.concatenate(parts, axis=0)
    irows = N_KV * n_new
    imp = jnp.dot(p_sum, ov_ref[...], preferred_element_type=F32, precision=lax.Precision.HIGHEST)
    jj = lax.broadcasted_iota(jnp.int32, (irows, n_selp), 1)
    cur = (past_len + lax.broadcasted_iota(jnp.int32, (irows, n_selp), 0) % n_new) // SEL_BLOCK
    forced = (jj == 0) | (jj == cur) | (jj == cur - 1)
    imp = jnp.where(forced, FORCE, imp)
    imp = jnp.where(jj <= cur, imp, NEG)
    sel = jnp.zeros((irows, n_selp), F32)
    n_sel = (past_len + n_new + SEL_BLOCK - 1) // SEL_BLOCK
    for _ in range(min(N_SEL, n_sel)):
        mx = jnp.max(imp, axis=1, keepdims=True)
        first = jnp.min(jnp.where(imp == mx, jj, n_selp), axis=1, keepdims=True)
        pick = jj == first
        sel = jnp.where(pick, 1.0, sel)
        imp = jnp.where(pick, BELOW_NEG, imp)
    sel_b = sel.astype(BF16)

    def attend(carry, kt, vt, k0, width):
        m, l, acc = carry
        st = jnp.dot(qb, kt, preferred_element_type=F32)
        key = k0 + lax.broadcasted_iota(jnp.int32, (1, width), 1)
        expand = lax.broadcasted_iota(jnp.int32, (n_selp, width), 0) == key // SEL_BLOCK
        picked = jnp.dot(sel_b, jnp.where(expand, 1.0, 0.0).astype(BF16), preferred_element_type=F32)
        picked = jnp.concatenate(
            [picked[g * n_new:(g + 1) * n_new] for g in range(N_KV) for _ in range(GROUP)], axis=0)
        maskf = jnp.where(key <= q_pos, picked, 0.0)
        st = jnp.where(maskf > 0.0, st, NEG)
        m_new = jnp.maximum(m, jnp.max(st, axis=-1, keepdims=True))
        alpha = jnp.exp(m - m_new)
        p = jnp.exp(st - m_new) * maskf
        l = alpha * l + jnp.sum(p, axis=-1, keepdims=True)
        acc = alpha * acc + _dot_nt(p.astype(BF16), vt)
        return m_new, l, acc

    _wait_pages(pool_k, kbuf, slot, ksem, n_pages)
    _wait_pages(pool_v, vbuf, slot, vsem, n_pages)

    def sel_tile(c, carry):
        pages = lambda buf: jnp.concatenate(
            [buf[slot, c * PAGES_PER_TILE + p].astype(BF16) for p in range(PAGES_PER_TILE)], axis=1)
        return attend(carry, pages(kbuf), pages(vbuf), c * SEL_TK, SEL_TK)

    carry = (jnp.full((rows, 1), NEG, F32), jnp.zeros((rows, 1), F32), jnp.zeros((rows, KV_COLS), F32))
    carry = lax.fori_loop(0, n_pages // PAGES_PER_TILE, sel_tile, carry)
    _, l_s, acc_s = attend(carry, ksn_ref[0], vsn_ref[0], past_len, ksn_ref.shape[2])
    o_s = acc_s / jnp.maximum(l_s, 1e-30)

    kwt = jnp.concatenate([kwc_ref[0].astype(BF16), kwn_ref[0]], axis=1)
    vwt = jnp.concatenate([vwc_ref[0].astype(BF16), vwn_ref[0]], axis=1)
    sw = jnp.dot(qb, kwt, preferred_element_type=F32)
    kpos = past_len - win_buf + lax.broadcasted_iota(jnp.int32, (1, kwt.shape[1]), 1)
    dist = q_pos - kpos
    p_w = _softmax_rows(sw, (dist >= 0) & (dist < WINDOW) & (kpos >= 0))
    o_w = _dot_nt(p_w.astype(BF16), vwt)

    gt = gates_ref[0]
    o = gt[:, 0:1] * o_c + gt[:, 1:2] * o_s + gt[:, 2:3] * o_w
    diag = lax.broadcasted_iota(jnp.int32, (rows, KV_COLS), 1) // HEAD_DIM == row // grows
    o_ref[0] = jnp.where(diag, o, 0.0)


def _nsa_sample_paged(page_table, pool_k, pool_v, qbd, kvcc, ksn, vsn, kwc, vwc, kwn, vwn, gate_rows, *, n_new, win_buf):
    n, n_pages = page_table.shape
    n_cmp = kvcc.shape[2]
    n_sel = (n_pages * PAGE_SIZE + n_new + SEL_BLOCK - 1) // SEL_BLOCK
    n_selp = (n_sel + LANES - 1) // LANES * LANES
    assert n_pages % PAGES_PER_TILE == 0
    ov = jnp.asarray(_overlap_matrix(n_cmp, n_selp))
    per = lambda a: pl.BlockSpec((1,) + a.shape[1:], lambda i, pt: (i,) + (0,) * (a.ndim - 1))
    grid_spec = pltpu.PrefetchScalarGridSpec(
        num_scalar_prefetch=1,
        grid=(n,),
        in_specs=[
            pl.BlockSpec(memory_space=pl.ANY), pl.BlockSpec(memory_space=pl.ANY),
            per(qbd),
            pl.BlockSpec((1, 1) + kvcc.shape[2:], lambda i, pt: (0, i, 0, 0)),
            pl.BlockSpec((1, 1) + kvcc.shape[2:], lambda i, pt: (1, i, 0, 0)),
            per(ksn), per(vsn), per(kwc), per(vwc), per(kwn), per(vwn), per(gate_rows),
            pl.BlockSpec(ov.shape, lambda i, pt: (0, 0)),
        ],
        out_specs=pl.BlockSpec((1, N_HEADS * n_new, KV_COLS), lambda i, pt: (i, 0, 0)),
        scratch_shapes=[pltpu.VMEM((2, n_pages, KV_COLS, PAGE_SIZE), F32),
                        pltpu.VMEM((2, n_pages, KV_COLS, PAGE_SIZE), F32),
                        pltpu.SemaphoreType.DMA((2,)), pltpu.SemaphoreType.DMA((2,))],
    )
    return pl.pallas_call(
        functools.partial(_nsa_sample_paged_body, n_pages=n_pages, n_new=n_new, win_buf=win_buf),
        grid_spec=grid_spec,
        out_shape=jax.ShapeDtypeStruct((n, N_HEADS * n_new, KV_COLS), F32),
        compiler_params=_params("arbitrary"),
        name="nsa_sample_paged",
    )(page_table, pool_k, pool_v, qbd, kvcc, kvcc, ksn, vsn, kwc, vwc, kwn, vwn, gate_rows, ov)


def _merge_body(x_ref, on_ref, gu_ref, v_ref, sga_ref, sgb_ref, wsg_ref, bsg_ref,
                wpa_ref, wpb_ref, wo_ref, g1_ref, b1_ref, wr_ref, br_ref,
                h_ref, hb_ref, route_ref, *, alpha):
    tm = x_ref.shape[0]
    lane = lax.broadcasted_iota(jnp.int32, (CHUNK, LANES), 1)
    tri = lax.broadcasted_iota(jnp.int32, (CHUNK, CHUNK), 0) >= lax.broadcasted_iota(jnp.int32, (CHUNK, CHUNK), 1)
    zero = jnp.zeros((CHUNK, CHUNK), BF16)
    chunks = []
    for c in range(tm // CHUNK):
        rs = slice(c * CHUNK, (c + 1) * CHUNK)
        cols = []
        for p in range(A_GROUPS // 2):
            cs = slice(p * LANES, (p + 1) * LANES)
            v2 = v_ref[rs, cs].astype(BF16)
            lo = jnp.dot(jnp.where(tri, wsg_ref[0, 2 * p], zero), v2, preferred_element_type=F32)
            hi = jnp.dot(jnp.where(tri, wsg_ref[0, 2 * p + 1], zero), v2, preferred_element_type=F32)
            mixed = jnp.where(lane < A_GDIM, lo, hi) + bsg_ref[0, :, cs]
            cols.append(gu_ref[rs, cs] * mixed)
        chunks.append(jnp.concatenate(cols, axis=1))
    a_out = jnp.concatenate(chunks, axis=0).astype(BF16)
    m = (sga_ref[...] * jnp.dot(a_out, wpa_ref[...], preferred_element_type=F32)
         + sgb_ref[...] * jnp.dot(on_ref[...], wpb_ref[...], preferred_element_type=F32))
    pre = alpha * x_ref[...] + jnp.dot(m.astype(BF16), wo_ref[...], preferred_element_type=F32)
    h = _layer_norm_rows(pre, g1_ref[...], b1_ref[...])
    h_ref[...] = h
    hb_ref[...] = h.astype(BF16)

    lg = jnp.dot(h, wr_ref[...], preferred_element_type=F32, precision=lax.Precision.HIGHEST) + br_ref[...]
    col = lax.broadcasted_iota(jnp.int32, (tm, LANES), 1)
    is_g = col < N_GROUPS
    gl = jnp.where(is_g, lg, BELOW_NEG)
    gmax = jnp.max(gl, axis=1, keepdims=True)
    grp = jnp.min(jnp.where(gl == gmax, col, LANES), axis=1, keepdims=True)
    p_grp = 1.0 / jnp.sum(jnp.where(is_g, jnp.exp(gl - gmax), 0.0), axis=1, keepdims=True)
    e_idx = col - N_GROUPS
    in_grp = (e_idx >= 0) & (e_idx < N_EXPERTS) & (e_idx // EXPERTS_PER_GROUP == grp)
    el = jnp.where(in_grp, lg, BELOW_NEG)
    m1 = jnp.max(el, axis=1, keepdims=True)
    i1 = jnp.min(jnp.where(el == m1, col, LANES), axis=1, keepdims=True)
    el2 = jnp.where(col == i1, BELOW_NEG, el)
    m2 = jnp.max(el2, axis=1, keepdims=True)
    i2 = jnp.min(jnp.where(el2 == m2, col, LANES), axis=1, keepdims=True)
    e2 = jnp.exp(m2 - m1)
    w1 = p_grp / (1.0 + e2)
    w2 = p_grp * e2 / (1.0 + e2)
    route = jnp.where(col == 0, (i1 - N_GROUPS).astype(F32),
                      jnp.where(col == 1, (i2 - N_GROUPS).astype(F32),
                                jnp.where(col == 2, w1, jnp.where(col == 3, w2, 0.0))))
    route_ref[...] = route


def _merge(x, o_nsa, gu, v, sga, sgb, wsg, bsg, wpa, wpb, wo, g1, b1, wr, br, *, alpha, n_first):
    t = x.shape[0]
    tm = MERGE_TM
    assert t % tm == 0
    row = lambda n: pl.BlockSpec((tm, n), lambda i: (i, 0))
    full = lambda a: pl.BlockSpec(a.shape, lambda i: (0,) * a.ndim)
    pick = lambda a: pl.BlockSpec((1,) + a.shape[1:], lambda i: (jnp.where(i < n_first, 0, 1),) + (0,) * (a.ndim - 1))
    return pl.pallas_call(
        functools.partial(_merge_body, alpha=alpha),
        grid=(t // tm,),
        in_specs=[row(D_MODEL), row(D_MODEL), row(A_HALF), row(A_HALF), row(D_MODEL), row(D_MODEL),
                  pick(wsg), pick(bsg), full(wpa), full(wpb), full(wo), full(g1), full(b1), full(wr), full(br)],
        out_specs=[row(D_MODEL), row(D_MODEL), row(LANES)],
        out_shape=[jax.ShapeDtypeStruct((t, D_MODEL), F32), jax.ShapeDtypeStruct((t, D_MODEL), BF16),
                   jax.ShapeDtypeStruct((t, LANES), F32)],
        compiler_params=_params("parallel"),
        name="merge",
    )(x, o_nsa, gu, v, sga, sgb, wsg, bsg, wpa, wpb, wo, g1, b1, wr, br)


def _experts_body(be_ref, xs_ref, sw_ref, wg_ref, wu_ref, wd_ref, o_ref):
    del be_ref
    x = xs_ref[...]
    gate = jnp.dot(x, wg_ref[0], preferred_element_type=F32)
    up = jnp.dot(x, wu_ref[0], preferred_element_type=F32)
    hid = (jax.nn.silu(gate) * up).astype(BF16)
    o_ref[...] = jnp.dot(hid, wd_ref[0], preferred_element_type=F32) * sw_ref[...]


def _experts(blk_exp, xs, slot_w, wg, wu, wd):
    n_slots = xs.shape[0]
    bm = MOE_BM
    grid_spec = pltpu.PrefetchScalarGridSpec(
        num_scalar_prefetch=1,
        grid=(n_slots // bm,),
        in_specs=[
            pl.BlockSpec((bm, D_MODEL), lambda i, be: (i, 0)),
            pl.BlockSpec((bm, 1), lambda i, be: (i, 0)),
            pl.BlockSpec((1, D_MODEL, D_EXPERT), lambda i, be: (be[i], 0, 0)),
            pl.BlockSpec((1, D_MODEL, D_EXPERT), lambda i, be: (be[i], 0, 0)),
            pl.BlockSpec((1, D_EXPERT, D_MODEL), lambda i, be: (be[i], 0, 0)),
        ],
        out_specs=pl.BlockSpec((bm, D_MODEL), lambda i, be: (i, 0)),
    )
    return pl.pallas_call(
        _experts_body,
        grid_spec=grid_spec,
        out_shape=jax.ShapeDtypeStruct((n_slots, D_MODEL), F32),
        compiler_params=_params("arbitrary"),
        name="experts",
    )(blk_exp, xs, slot_w, wg, wu, wd)


def _final_body(h_ref, y_ref, g_ref, b_ref, o_ref, *, alpha):
    o_ref[...] = _layer_norm_rows(alpha * h_ref[...] + y_ref[...], g_ref[...], b_ref[...])


def _final_norm(h, y, g, b, *, alpha):
    t = h.shape[0]
    tm = MERGE_TM
    row = pl.BlockSpec((tm, D_MODEL), lambda i: (i, 0))
    full = lambda a: pl.BlockSpec(a.shape, lambda i: (0,) * a.ndim)
    return pl.pallas_call(
        functools.partial(_final_body, alpha=alpha),
        grid=(t // tm,),
        in_specs=[row, row, full(g), full(b)],
        out_specs=row,
        out_shape=jax.ShapeDtypeStruct((t, D_MODEL), F32),
        compiler_params=_params("parallel"),
        name="final_norm",
    )(h, y, g, b)


def _moe_dispatch(route):
    t = route.shape[0]
    expert = route[:, 0:TOP_K].astype(jnp.int32)
    w_top = route[:, TOP_K:2 * TOP_K]
    n_assign = t * TOP_K
    flat_e = expert.reshape(-1)
    flat_tok = jnp.arange(n_assign, dtype=jnp.int32) // TOP_K
    flat_w = w_top.reshape(-1)
    order = jnp.argsort(flat_e)
    e_s = flat_e[order]
    counts = jnp.zeros((N_EXPERTS,), jnp.int32).at[flat_e].add(1)
    padded = (counts + MOE_BM - 1) // MOE_BM * MOE_BM
    pad_end = jnp.cumsum(padded)
    pad_start = pad_end - padded
    start = jnp.cumsum(counts) - counts
    dest_s = pad_start[e_s] + jnp.arange(n_assign, dtype=jnp.int32) - start[e_s]
    n_blocks = -(-n_assign // MOE_BM) + N_EXPERTS
    n_slots = n_blocks * MOE_BM
    slot_tok = jnp.zeros((n_slots,), jnp.int32).at[dest_s].set(flat_tok[order])
    slot_w = jnp.zeros((n_slots,), F32).at[dest_s].set(flat_w[order])
    dest = jnp.zeros((n_assign,), jnp.int32).at[order].set(dest_s).reshape(t, TOP_K)
    blk_start = jnp.arange(n_blocks, dtype=jnp.int32) * MOE_BM
    blk_exp = jnp.minimum(jnp.sum(pad_end[None, :] <= blk_start[:, None], axis=1), N_EXPERTS - 1).astype(jnp.int32)
    return slot_tok, slot_w, dest, blk_exp


def _nsa_sample_glue(page_table, cmp_k, cmp_v, sel_k, sel_v, win_k, win_v, q, kc, vc, ksb, vsb, kwb, vwb, gates, cw, *, n_new):
    n, n_pages = page_table.shape
    ts = n * n_new
    win_buf = win_k.shape[1]
    new = lambda a: a.reshape(n, n_new, KV_COLS)
    cmp_out = (n_pages * (PAGE_SIZE // CMP_STRIDE) + 16 + LANES - 1) // LANES * LANES
    kvcc = _compress_paged(page_table, _page_view(cmp_k), _page_view(cmp_v), jnp.stack([new(kc), new(vc)]), cw, out_rows=cmp_out)
    new_t = lambda a: jnp.pad(new(a).transpose(0, 2, 1), ((0, 0), (0, 0), (0, LANES - n_new)))
    q_s = q.reshape(n, n_new, N_KV, GROUP, HEAD_DIM)
    qbd = jnp.einsum('ntgrd,gj->ngrtjd', q_s, jnp.eye(N_KV, dtype=BF16)).reshape(n, N_HEADS * n_new, KV_COLS)
    g_s = gates.reshape(n, n_new, N_KV, LANES)[..., :3 * GROUP].reshape(n, n_new, N_KV, 3, GROUP)
    gate_rows = jnp.pad(g_s.transpose(0, 2, 4, 1, 3).reshape(n, N_HEADS * n_new, 3), ((0, 0), (0, 0), (0, LANES - 3)))
    od = _nsa_sample_paged(page_table, _page_view(sel_k), _page_view(sel_v), qbd, kvcc,
                           new_t(ksb), new_t(vsb), _page_view(win_k), _page_view(win_v),
                           new_t(kwb), new_t(vwb), gate_rows, n_new=n_new, win_buf=win_buf)
    o_s = jnp.einsum('ngrtgd->ntgrd', od.reshape(n, N_KV, GROUP, n_new, N_KV, HEAD_DIM))
    return o_s.reshape(ts, N_HEADS * HEAD_DIM).astype(BF16)


def _sgu_weights(sgu_w, sgu_b, n_new):
    reps = CHUNK // n_new
    w_short = jnp.einsum('ab,gts->gatbs', jnp.eye(reps, dtype=F32), sgu_w[:, :n_new, :n_new]).reshape(A_GROUPS, CHUNK, CHUNK)
    wsg = jnp.stack([sgu_w[:, :CHUNK, :CHUNK], w_short]).astype(BF16)
    b_full = jnp.repeat(sgu_b[:, :CHUNK].T, A_GDIM, axis=1)
    b_short = jnp.tile(jnp.repeat(sgu_b[:, :n_new].T, A_GDIM, axis=1), (reps, 1))
    return wsg, jnp.stack([b_full, b_short])


def kernel(x_prompt, x_sample, cache_cmp_k, cache_cmp_v, cache_sel_k, cache_sel_v, cache_win_k, cache_win_v, page_table, w_in, cmp_pe, cmp_w1, cmp_b1, cmp_w2, cmp_b2, sgu_ln_g, sgu_ln_b, sgu_w, sgu_b, w_proj_a, w_proj_b, w_out, ln1_g, ln1_b, router_group_w, router_group_b, router_expert_w, router_expert_b, exp_w_gate, exp_w_up, exp_w_down, ln2_g, ln2_b):
    depth = w_in.shape[0]
    alpha = (2.0 * depth) ** 0.25
    b, t, _ = x_prompt.shape
    n, n_new, _ = x_sample.shape
    tp, ts = b * t, n * n_new
    past_len = page_table.shape[1] * PAGE_SIZE
    win_buf = cache_win_k.shape[2]
    assert CHUNK % n_new == 0 and tp % MERGE_TM == 0 and ts % MERGE_TM == 0

    x_p, x_s = x_prompt, x_sample
    states = []
    for l in range(depth):
        x_all = jnp.concatenate([x_p.reshape(tp, D_MODEL), x_s.reshape(ts, D_MODEL)], axis=0)
        (q, kc, vc, ks, vs, kw, vw, ksb, vsb, kwb, vwb, gu, v, sga, sgb, gates) = _in_proj(
            x_all, _pack_w_in(w_in[l]), sgu_ln_g[l][None], sgu_ln_b[l][None])
        cw = _compress_weights(cmp_pe[l], cmp_w1[l], cmp_b1[l], cmp_w2[l], cmp_b2[l])

        sub_w = CMP_STRIDE * KV_COLS
        kcc, vcc = _compress(kc[:tp].reshape(b, t // CMP_STRIDE, sub_w), vc[:tp].reshape(b, t // CMP_STRIDE, sub_w),
                             cw, tile=128, out_rows=t // CMP_STRIDE)
        seq = lambda a: a[:tp].reshape(b, t, -1)
        o_p = _nsa_prompt_glue(seq(q), kcc, vcc, seq(ksb), seq(vsb), seq(kwb), seq(vwb), seq(gates))

        tail = lambda a: a[tp:]
        o_s = _nsa_sample_glue(page_table, cache_cmp_k[l], cache_cmp_v[l], cache_sel_k[l], cache_sel_v[l],
                               cache_win_k[l], cache_win_v[l], tail(q), tail(kc), tail(vc), tail(ksb), tail(vsb),
                               tail(kwb), tail(vwb), tail(gates), cw, n_new=n_new)
        o_all = jnp.concatenate([o_p.reshape(tp, -1), o_s], axis=0)

        wsg, bsg = _sgu_weights(sgu_w[l], sgu_b[l], n_new)
        wr = jnp.pad(jnp.concatenate([router_group_w[l], router_expert_w[l]], axis=1), ((0, 0), (0, LANES - N_GROUPS - N_EXPERTS)))
        br = jnp.pad(jnp.concatenate([router_group_b[l], router_expert_b[l]]), (0, LANES - N_GROUPS - N_EXPERTS))[None]
        h, hb, route = _merge(x_all, o_all, gu, v, sga, sgb, wsg, bsg,
                              w_proj_a[l].astype(BF16), w_proj_b[l].astype(BF16), w_out[l].astype(BF16),
                              ln1_g[l][None], ln1_b[l][None], wr, br, alpha=alpha, n_first=tp // MERGE_TM)

        slot_tok, slot_w, dest, blk_exp = _moe_dispatch(route)
        out = _experts(blk_exp, hb[slot_tok], slot_w[:, None],
                       exp_w_gate[l].astype(BF16), exp_w_up[l].astype(BF16), exp_w_down[l].astype(BF16))
        y_moe = out[dest[:, 0]] + out[dest[:, 1]]
        y_all = _final_norm(h, y_moe, ln2_g[l][None], ln2_b[l][None], alpha=alpha)

        kv5 = lambda a, lo, hi, bb: a[lo:hi].reshape(bb, -1, N_KV, HEAD_DIM)
        keep_p = min(WINDOW, t)
        prompt_state = tuple(kv5(a, 0, tp, b) for a in (kc, vc, ks, vs)) + tuple(
            kv5(a, 0, tp, b)[:, t - keep_p:] for a in (kw, vw))
        keep_s = min(WINDOW, win_buf + n_new)
        win_all = lambda cache, a: jnp.concatenate([cache[l], kv5(a, tp, tp + ts, n)], axis=1)[:, win_buf + n_new - keep_s:]
        sample_state = tuple(kv5(a, tp, tp + ts, n) for a in (kc, vc, ks, vs)) + (
            win_all(cache_win_k, kw), win_all(cache_win_v, vw), v[tp:].reshape(n, n_new, A_HALF))
        states.append(prompt_state + sample_state)
        x_p, x_s = y_all[:tp].reshape(b, t, D_MODEL), y_all[tp:].reshape(n, n_new, D_MODEL)
    return (x_p, x_s) + tuple(jnp.stack(z) for z in zip(*states))
```

```python
import functools

import numpy as np
import jax
import jax.numpy as jnp
from jax import lax
from jax.experimental import pallas as pl
from jax.experimental.pallas import tpu as pltpu

F32 = jnp.float32
BF16 = jnp.bfloat16

D_MODEL = 1024
N_HEADS = 16
HEAD_DIM = 64
N_KV = 4
GROUP = N_HEADS // N_KV
KV_COLS = N_KV * HEAD_DIM
CMP_STRIDE = 16
CMP_BLOCK = 32
CMP_HIDDEN = 64
SEL_BLOCK = 64
N_SEL = 16
WINDOW = 512
Q_BLOCK = 128
PAGE_SIZE = 128
CHUNK = 128
A_GROUPS = 8
A_HALF = D_MODEL // 2
A_GDIM = A_HALF // A_GROUPS
N_GROUPS = 4
EXPERTS_PER_GROUP = 8
N_EXPERTS = N_GROUPS * EXPERTS_PER_GROUP
TOP_K = 2
D_EXPERT = D_MODEL // 2
LN_EPS = 1e-5
NEG = -1e30
FORCE = 1e9
BELOW_NEG = -3e38

LANES = 128
VMEM_LIMIT = 56 * 1024 * 1024

INPROJ_TM = 256
MERGE_TM = 256
MOE_BM = 256
SEL_TK = 512
SEL_UNROLL = 4


def _gelu(x):
    return jax.nn.gelu(x, approximate=True)


def _layer_norm_rows(x, g, b):
    mu = jnp.mean(x, axis=-1, keepdims=True)
    xc = x - mu
    var = jnp.mean(xc * xc, axis=-1, keepdims=True)
    return xc * lax.rsqrt(var + LN_EPS) * g + b


def _params(*sem):
    return pltpu.CompilerParams(dimension_semantics=tuple(sem), vmem_limit_bytes=VMEM_LIMIT)


_C_Q = 0
_C_KV = D_MODEL
_C_AU = _C_KV + 6 * KV_COLS
_C_AV = _C_AU + A_HALF
_C_GA = _C_AV + A_HALF
_C_GB = _C_GA + D_MODEL
_C_GATE = _C_GB + D_MODEL
_C_END = _C_GATE + N_KV * LANES


def _pack_w_in(w_in):
    q_cols = N_HEADS * HEAD_DIM
    o_gate = q_cols + 6 * KV_COLS
    o_au = o_gate + 3 * N_HEADS
    src = np.zeros((N_KV * LANES,), np.int32)
    valid = np.zeros((N_KV * LANES,), np.float32)
    for g in range(N_KV):
        for c in range(3):
            for r in range(GROUP):
                src[g * LANES + c * GROUP + r] = o_gate + (g * GROUP + r) * 3 + c
                valid[g * LANES + c * GROUP + r] = 1.0
    w_gate = w_in[:, src] * jnp.asarray(valid)
    packed = jnp.concatenate([w_in[:, :o_gate], w_in[:, o_au:], w_gate], axis=1)
    return packed.astype(BF16)


def _inproj_body(x_ref, w_ref, lng_ref, lnb_ref,
                 q_ref, kc_ref, vc_ref, ks_ref, vs_ref, kw_ref, vw_ref,
                 ksb_ref, vsb_ref, kwb_ref, vwb_ref,
                 gu_ref, v_ref, sga_ref, sgb_ref, gates_ref):
    x = x_ref[...].astype(BF16)

    def mm(c0, n):
        return jnp.dot(x, w_ref[:, c0:c0 + n], preferred_element_type=F32)

    q_ref[...] = (mm(_C_Q, D_MODEL) * (HEAD_DIM ** -0.5)).astype(BF16)
    f32_refs = (kc_ref, vc_ref, ks_ref, vs_ref, kw_ref, vw_ref)
    bf_refs = (None, None, ksb_ref, vsb_ref, kwb_ref, vwb_ref)
    for i in range(6):
        y = mm(_C_KV + i * KV_COLS, KV_COLS)
        f32_refs[i][...] = y
        if bf_refs[i] is not None:
            bf_refs[i][...] = y.astype(BF16)
    gu_ref[...] = _gelu(mm(_C_AU, A_HALF))
    v_ref[...] = _layer_norm_rows(_gelu(mm(_C_AV, A_HALF)), lng_ref[...], lnb_ref[...])
    sga_ref[...] = jax.nn.sigmoid(mm(_C_GA, D_MODEL))
    sgb_ref[...] = jax.nn.sigmoid(mm(_C_GB, D_MODEL))
    gates_ref[...] = jax.nn.sigmoid(mm(_C_GATE, N_KV * LANES))


def _in_proj(x, w_packed, ln_g, ln_b):
    t = x.shape[0]
    tm = INPROJ_TM
    assert t % tm == 0
    row = lambda n: pl.BlockSpec((tm, n), lambda i: (i, 0))
    full = lambda a: pl.BlockSpec(a.shape, lambda i: (0,) * a.ndim)
    out_shapes = (
        [jax.ShapeDtypeStruct((t, D_MODEL), BF16)]
        + [jax.ShapeDtypeStruct((t, KV_COLS), F32)] * 6
        + [jax.ShapeDtypeStruct((t, KV_COLS), BF16)] * 4
        + [jax.ShapeDtypeStruct((t, A_HALF), F32)] * 2
        + [jax.ShapeDtypeStruct((t, D_MODEL), F32)] * 2
        + [jax.ShapeDtypeStruct((t, N_KV * LANES), F32)]
    )
    out_specs = [row(s.shape[1]) for s in out_shapes]
    return pl.pallas_call(
        _inproj_body,
        grid=(t // tm,),
        in_specs=[row(D_MODEL), full(w_packed), full(ln_g), full(ln_b)],
        out_specs=out_specs,
        out_shape=out_shapes,
        compiler_params=_params("parallel"),
        name="in_proj",
    )(x, w_packed, ln_g, ln_b)


def _compress_weights(pe, w1, b1, w2, b2):
    eye = jnp.eye(N_KV, dtype=F32)
    w1r = w1.reshape(2, 2, CMP_STRIDE, HEAD_DIM, CMP_HIDDEN)
    w1bd = jnp.einsum('kfsdh,gj->kfsgdjh', w1r, eye).reshape(
        2, 2, CMP_STRIDE * KV_COLS, N_KV * CMP_HIDDEN).astype(BF16)
    per = pe.reshape(2, 2, CMP_STRIDE, 1, HEAD_DIM)
    pe_row = jnp.broadcast_to(per, (2, 2, CMP_STRIDE, N_KV, HEAD_DIM)).reshape(2, 2, 1, CMP_STRIDE * KV_COLS)
    b1_row = jnp.tile(b1, (1, N_KV)).reshape(2, 1, N_KV * CMP_HIDDEN)
    w2bd = jnp.einsum('khd,gj->kghjd', w2, eye).reshape(2, N_KV * CMP_HIDDEN, KV_COLS).astype(BF16)
    b2_row = jnp.tile(b2, (1, N_KV)).reshape(2, 1, KV_COLS)
    w1pair = jnp.einsum('kfsdh,ab->ksadfbh', w1r, jnp.eye(2, dtype=F32)).reshape(
        2, CMP_STRIDE * KV_COLS // 2, 2 * 2 * CMP_HIDDEN).astype(BF16)
    return pe_row, w1bd, b1_row, w2bd, b2_row, w1pair


def _compress_body(xk_ref, xv_ref, pe_ref, w1_ref, b1_ref, w2_ref, b2_ref, ok_ref, ov_ref,
                   first_ref, second_ref, *, rows, tile, out_rows):
    j = pl.program_id(1)
    r0 = pl.multiple_of(j * tile, 8)
    for kv, x_ref in enumerate((xk_ref, xv_ref)):
        x = x_ref[0]
        first_ref[kv, pl.ds(r0, tile), :] = jnp.dot(
            (x + pe_ref[kv, 0]).astype(BF16), w1_ref[kv, 0], preferred_element_type=F32)
        second_ref[kv, pl.ds(r0, tile), :] = jnp.dot(
            (x + pe_ref[kv, 1]).astype(BF16), w1_ref[kv, 1], preferred_element_type=F32)

    @pl.when(j == pl.num_programs(1) - 1)
    def _():
        for kv, o_ref in enumerate((ok_ref, ov_ref)):
            nxt = pltpu.roll(second_ref[kv], rows - 1, axis=0)
            hid = _gelu(first_ref[kv] + nxt + b1_ref[kv])
            out = jnp.dot(hid.astype(BF16), w2_ref[kv], preferred_element_type=F32) + b2_ref[kv]
            o_ref[0, 0:rows, :] = out.astype(BF16)
            if out_rows > rows:
                o_ref[0, rows:out_rows, :] = jnp.zeros((out_rows - rows, KV_COLS), BF16)


def _compress(xk, xv, cw, *, tile, out_rows):
    pe_row, w1bd, b1_row, w2bd, b2_row, _ = cw
    n, rows, width = xk.shape
    assert rows % tile == 0 and tile % 8 == 0
    xspec = pl.BlockSpec((1, tile, width), lambda i, j: (i, j, 0))
    full = lambda a: pl.BlockSpec(a.shape, lambda i, j: (0,) * a.ndim)
    ospec = pl.BlockSpec((1, out_rows, KV_COLS), lambda i, j: (i, 0, 0))
    oshape = jax.ShapeDtypeStruct((n, out_rows, KV_COLS), BF16)
    return pl.pallas_call(
        functools.partial(_compress_body, rows=rows, tile=tile, out_rows=out_rows),
        grid=(n, rows // tile),
        in_specs=[xspec, xspec, full(pe_row), full(w1bd), full(b1_row), full(w2bd), full(b2_row)],
        out_specs=[ospec, ospec],
        out_shape=[oshape, oshape],
        scratch_shapes=[pltpu.VMEM((2, rows, N_KV * CMP_HIDDEN), F32),
                        pltpu.VMEM((2, rows, N_KV * CMP_HIDDEN), F32)],
        compiler_params=_params("parallel", "arbitrary"),
        name="compress",
    )(xk, xv, pe_row, w1bd, b1_row, w2bd, b2_row)


def _softmax_rows(s, mask):
    s = jnp.where(mask, s, NEG)
    e = jnp.exp(s - jnp.max(s, axis=-1, keepdims=True)) * mask.astype(F32)
    return e / jnp.maximum(jnp.sum(e, axis=-1, keepdims=True), 1e-30)


def _dot_nt(a, b, **kw):
    return lax.dot_general(a, b, (((1,), (1,)), ((), ())), preferred_element_type=F32, **kw)


def _overlap_matrix(n_cmp_rows, n_sel_cols):
    c_start = np.arange(n_cmp_rows)[:, None] * CMP_STRIDE
    s_start = np.arange(n_sel_cols)[None, :] * SEL_BLOCK
    return ((c_start < s_start + SEL_BLOCK) & (c_start + CMP_BLOCK > s_start)).astype(np.float32)


def _nsa_prompt_body(q_ref, kcc_ref, vcc_ref, ks_ref, vs_ref, kw_ref, vw_ref, gates_ref, ovt_ref, wb_ref, o_ref):
    i = pl.program_id(2)
    tq = Q_BLOCK
    rows = GROUP * tq
    n_cmp = kcc_ref.shape[2]
    n_sel = ovt_ref.shape[0]
    q2 = q_ref[0]
    qs = jnp.concatenate([q2[:, r * HEAD_DIM:(r + 1) * HEAD_DIM] for r in range(GROUP)], axis=0)
    tok = lax.broadcasted_iota(jnp.int32, (rows, 1), 0) % tq
    q_pos = i * tq + tok

    cmp_end = lax.broadcasted_iota(jnp.int32, (1, n_cmp), 1) * CMP_STRIDE + (CMP_BLOCK - 1)
    s = jnp.where(cmp_end <= q_pos, _dot_nt(qs, kcc_ref[0, 0]), NEG)
    e_c = jnp.exp(s - jnp.max(s, axis=-1, keepdims=True))
    p_c = e_c * jnp.where(q_pos >= CMP_BLOCK - 1, 1.0 / jnp.sum(e_c, axis=-1, keepdims=True), 0.0)
    o_c = jnp.dot(p_c.astype(BF16), vcc_ref[0, 0], preferred_element_type=F32)

    wk = WINDOW + tq
    w0 = pl.multiple_of(i * tq, tq)
    kpos = i * tq - WINDOW + lax.broadcasted_iota(jnp.int32, (1, wk), 1)
    sw = _dot_nt(qs, kw_ref[0, 0, pl.ds(w0, wk), :]) + wb_ref[...] + jnp.where(kpos >= 0, 0.0, NEG)
    e_w = jnp.exp(sw - jnp.max(sw, axis=-1, keepdims=True))
    p_w = e_w * (1.0 / jnp.sum(e_w, axis=-1, keepdims=True))
    o_w = jnp.dot(p_w.astype(BF16), vw_ref[0, 0, pl.ds(w0, wk), :], preferred_element_type=F32)

    gt = gates_ref[0]
    gcol = lambda c: jnp.concatenate([gt[:, c * GROUP + r:c * GROUP + r + 1] for r in range(GROUP)], axis=0)
    o_cw = gcol(0) * o_c + gcol(2) * o_w

    p_sum = p_c[0:tq]
    for r in range(1, GROUP):
        p_sum = p_sum + p_c[r * tq:(r + 1) * tq]
    imp = _dot_nt(ovt_ref[...], p_sum, precision=lax.Precision.HIGHEST)
    jj = lax.broadcasted_iota(jnp.int32, (n_sel, tq), 0)
    cur = (i * tq + lax.broadcasted_iota(jnp.int32, (n_sel, tq), 1)) // SEL_BLOCK
    forced = (jj == 0) | (jj == cur) | (jj == cur - 1)
    imp = jnp.where(forced, FORCE, imp)
    imp = jnp.where(jj <= cur, imp, NEG)
    sel = jnp.zeros((n_sel, tq), F32)
    for _ in range(min(N_SEL, n_sel)):
        mx = jnp.max(imp, axis=0, keepdims=True)
        first = jnp.min(jnp.where(imp == mx, jj, n_sel), axis=0, keepdims=True)
        pick = jj == first
        sel = jnp.where(pick, 1.0, sel)
        imp = jnp.where(pick, BELOW_NEG, imp)
    nb = jnp.where(sel.T > 0.0, 0.0, NEG).astype(BF16)
    qa = jnp.concatenate([jnp.concatenate([nb] * GROUP, axis=0), qs], axis=1)

    def sel_tile(c, carry, causal):
        m, l, acc = carry
        k0 = pl.multiple_of(c * SEL_TK, SEL_TK)
        st = _dot_nt(qa, ks_ref[0, 0, pl.ds(k0, SEL_TK), :])
        if causal:
            key = k0 + lax.broadcasted_iota(jnp.int32, (1, SEL_TK), 1)
            st = jnp.where(key <= q_pos, st, NEG)
        m_new = jnp.maximum(m, jnp.max(st, axis=-1, keepdims=True))
        alpha = jnp.exp(m - m_new)
        p = jnp.exp(st - m_new)
        l = alpha * l + jnp.sum(p, axis=-1, keepdims=True)
        acc = alpha * acc + jnp.dot(p.astype(BF16), vs_ref[0, 0, pl.ds(k0, SEL_TK), :], preferred_element_type=F32)
        return m_new, l, acc

    def tile_group(gi, carry):
        for u in range(SEL_UNROLL):
            carry = sel_tile(SEL_UNROLL * gi + u, carry, False)
        return carry

    carry = (jnp.full((rows, 1), NEG, F32), jnp.zeros((rows, 1), F32), jnp.zeros((rows, HEAD_DIM), F32))
    n_full = (i * tq) // SEL_TK
    n_groups = n_full // SEL_UNROLL
    carry = lax.fori_loop(0, n_groups, tile_group, carry)
    carry = lax.fori_loop(n_groups * SEL_UNROLL, n_full, functools.partial(sel_tile, causal=False), carry)
    _, l_s, acc_s = sel_tile(n_full, carry, True)
    o_s = acc_s / l_s

    o = o_cw + gcol(1) * o_s
    for r in range(GROUP):
        o_ref[0, :, r * HEAD_DIM:(r + 1) * HEAD_DIM] = o[r * tq:(r + 1) * tq].astype(BF16)


def _nsa_prompt(q, kcc, vcc, ks, vs, kw, vw, gates):
    b, t, _ = q.shape
    n_cmp = kcc.shape[2]
    n_sel = t // SEL_BLOCK
    assert t % SEL_TK == 0 and t % Q_BLOCK == 0 and ks.shape[3] == n_sel + HEAD_DIM
    ovt = jnp.asarray(_overlap_matrix(n_cmp, n_sel).T)
    tt = np.arange(GROUP * Q_BLOCK)[:, None] % Q_BLOCK
    col = np.arange(WINDOW + Q_BLOCK)[None, :]
    wb = jnp.asarray(np.where((col > tt) & (col <= tt + WINDOW), 0.0, NEG).astype(np.float32))
    per_bg = lambda a: pl.BlockSpec((1, 1) + a.shape[2:], lambda bi, g, i: (bi, g, 0, 0))
    return pl.pallas_call(
        _nsa_prompt_body,
        grid=(b, N_KV, t // Q_BLOCK),
        in_specs=[
            pl.BlockSpec((1, Q_BLOCK, GROUP * HEAD_DIM), lambda bi, g, i: (bi, i, g)),
            per_bg(kcc), per_bg(vcc), per_bg(ks), per_bg(vs), per_bg(kw), per_bg(vw),
            pl.BlockSpec((1, Q_BLOCK, LANES), lambda bi, g, i: (bi, i, g)),
            pl.BlockSpec(ovt.shape, lambda bi, g, i: (0, 0)),
            pl.BlockSpec(wb.shape, lambda bi, g, i: (0, 0)),
        ],
        out_specs=pl.BlockSpec((1, Q_BLOCK, GROUP * HEAD_DIM), lambda bi, g, i: (bi, i, g)),
        out_shape=jax.ShapeDtypeStruct((b, t, N_HEADS * HEAD_DIM), BF16),
        compiler_params=_params("parallel", "parallel", "arbitrary"),
        name="nsa_prompt",
    )(q, kcc, vcc, ks, vs, kw, vw, gates, ovt, wb)


def _to_heads(a):
    b, l, _ = a.shape
    return a.reshape(b, l, N_KV, HEAD_DIM).transpose(0, 2, 1, 3)


def _nsa_prompt_glue(q, kcc, vcc, ks, vs, kw, vw, gates):
    pad = lambda a: jnp.pad(a, ((0, 0), (WINDOW, 0), (0, 0)))
    b, t, _ = ks.shape
    n_sel = t // SEL_BLOCK
    onehot = (jnp.arange(t)[:, None] // SEL_BLOCK == jnp.arange(n_sel)[None, :]).astype(BF16)
    ks_aug = jnp.concatenate([jnp.broadcast_to(onehot, (b, N_KV, t, n_sel)), _to_heads(ks)], axis=3)
    return _nsa_prompt(q, _to_heads(kcc), _to_heads(vcc), ks_aug, _to_heads(vs),
                       _to_heads(pad(kw)), _to_heads(pad(vw)), gates)


def _page_view(pool):
    return jnp.transpose(pool, (0, 2, 3, 1)).reshape(pool.shape[0], KV_COLS, pool.shape[1])


def _page_copy(pool_ref, page, buf_ref, slot, j, sem_ref):
    return pltpu.make_async_copy(pool_ref.at[page], buf_ref.at[slot, j], sem_ref.at[slot])


def _start_pages(pt_ref, seq, pool_ref, buf_ref, slot, sem_ref, n_pages):
    def body(j, c):
        _page_copy(pool_ref, pt_ref[seq, j], buf_ref, slot, j, sem_ref).start()
        return c
    lax.fori_loop(0, n_pages, body, 0)


def _wait_pages(pool_ref, buf_ref, slot, sem_ref, n_pages):
    def body(j, c):
        _page_copy(pool_ref, 0, buf_ref, slot, j, sem_ref).wait()
        return c
    lax.fori_loop(0, n_pages, body, 0)


def _compress_paged_body(pt_ref, pool_k, pool_v, new_ref, perm_ref, w1_ref, b1_ref, w2_ref, b2_ref,
                         o_ref, pbuf, xsub, sem, *, n_pages, out_rows):
    kv = pl.program_id(0)
    i = pl.program_id(1)
    n_seq = pl.num_programs(1)
    step = kv * n_seq + i
    slot = step % 2

    def start(kv_idx, seq, slot_idx):
        @pl.when(kv_idx == 0)
        def _():
            _start_pages(pt_ref, seq, pool_k, pbuf, slot_idx, sem, n_pages)

        @pl.when(kv_idx == 1)
        def _():
            _start_pages(pt_ref, seq, pool_v, pbuf, slot_idx, sem, n_pages)

    @pl.when(step == 0)
    def _():
        start(kv, i, slot)

    @pl.when(i + 1 < n_seq)
    def _():
        start(kv, i + 1, 1 - slot)

    @pl.when((i + 1 == n_seq) & (kv == 0))
    def _():
        start(kv + 1, 0, 1 - slot)

    _wait_pages(pool_k, pbuf, slot, sem, n_pages)

    def pair(p, c):
        xt = jnp.concatenate([pbuf[slot, 2 * p].astype(BF16), pbuf[slot, 2 * p + 1].astype(BF16)], axis=1)
        y = _dot_nt(perm_ref[...], xt)
        r0 = pl.multiple_of(p * 16, 16)
        for s in range(CMP_STRIDE):
            xsub[pl.ds(r0, 16), s * KV_COLS:(s + 1) * KV_COLS] = y[s * 16:(s + 1) * 16].astype(BF16)
        return c
    lax.fori_loop(0, n_pages // 2, pair, 0, unroll=4)
    past_sub = n_pages * (PAGE_SIZE // CMP_STRIDE)
    rows = past_sub + 16
    xsub[past_sub:rows, :] = new_ref[0, 0]

    xs = xsub[...]
    half = KV_COLS // 2
    yp = []
    for p in range(2):
        xp = jnp.concatenate(
            [xs[:, s * KV_COLS + p * half:s * KV_COLS + (p + 1) * half] for s in range(CMP_STRIDE)], axis=1)
        yp.append(jnp.dot(xp, w1_ref[0], preferred_element_type=F32))
    y = jnp.concatenate([yp[0][:, :half], yp[1][:, :half], yp[0][:, half:], yp[1][:, half:]], axis=1)
    h4 = N_KV * CMP_HIDDEN
    pe_first = y[past_sub + 4:past_sub + 5, :h4] + y[past_sub + 5:past_sub + 6, :h4]
    pe_second = y[past_sub + 6:past_sub + 7, h4:] + y[past_sub + 7:past_sub + 8, h4:]
    first = y[:, :h4] + pe_first
    nxt = pltpu.roll(y[:, h4:] + pe_second, rows - 1, axis=0)
    hid = _gelu(first + nxt + b1_ref[0])
    out = jnp.dot(hid.astype(BF16), w2_ref[0], preferred_element_type=F32) + b2_ref[0]
    o_ref[0, 0, 0:rows, :] = out.astype(BF16)
    o_ref[0, 0, rows:out_rows, :] = jnp.zeros((out_rows - rows, KV_COLS), BF16)


def _compress_paged(page_table, pool_k, pool_v, new_kv, cw, *, out_rows):
    pe_row, w1bd, b1_row, w2bd, b2_row, w1pair = cw
    n, n_pages = page_table.shape
    n_new = new_kv.shape[2]
    assert n_pages % 2 == 0 and n_new <= CMP_STRIDE
    past_sub = n_pages * (PAGE_SIZE // CMP_STRIDE)
    rows = past_sub + 16
    sub_w = CMP_STRIDE * KV_COLS
    pe = pe_row.reshape(2, 2, sub_w)
    pe_hi = pe.astype(BF16)
    pe_lo = (pe - pe_hi.astype(F32)).astype(BF16)
    pe_rows = jnp.stack([pe_hi[:, 0], pe_lo[:, 0], pe_hi[:, 1], pe_lo[:, 1]], axis=1)
    new_row = jnp.pad(new_kv.reshape(2, n, 1, n_new * KV_COLS), ((0, 0), (0, 0), (0, 3), (0, sub_w - n_new * KV_COLS)))
    new_sub = jnp.concatenate(
        [new_row.astype(BF16), jnp.broadcast_to(pe_rows[:, None], (2, n, 4, sub_w)),
         jnp.zeros((2, n, 8, sub_w), BF16)], axis=2)
    w1cat = w1pair
    m = np.arange(256)
    perm = np.zeros((256, 256), np.float32)
    perm[m, (m % 16) * 16 + m // 16] = 1.0
    perm = jnp.asarray(perm, BF16)
    grid_spec = pltpu.PrefetchScalarGridSpec(
        num_scalar_prefetch=1,
        grid=(2, n),
        in_specs=[
            pl.BlockSpec(memory_space=pl.ANY), pl.BlockSpec(memory_space=pl.ANY),
            pl.BlockSpec((1, 1, 16, sub_w), lambda kv, i, pt: (kv, i, 0, 0)),
            pl.BlockSpec(perm.shape, lambda kv, i, pt: (0, 0)),
            pl.BlockSpec((1,) + w1cat.shape[1:], lambda kv, i, pt: (kv, 0, 0)),
            pl.BlockSpec((1,) + b1_row.shape[1:], lambda kv, i, pt: (kv, 0, 0)),
            pl.BlockSpec((1,) + w2bd.shape[1:], lambda kv, i, pt: (kv, 0, 0)),
            pl.BlockSpec((1,) + b2_row.shape[1:], lambda kv, i, pt: (kv, 0, 0)),
        ],
        out_specs=pl.BlockSpec((1, 1, out_rows, KV_COLS), lambda kv, i, pt: (kv, i, 0, 0)),
        scratch_shapes=[pltpu.VMEM((2, n_pages, KV_COLS, PAGE_SIZE), F32),
                        pltpu.VMEM((rows, sub_w), BF16),
                        pltpu.SemaphoreType.DMA((2,))],
    )
    return pl.pallas_call(
        functools.partial(_compress_paged_body, n_pages=n_pages, out_rows=out_rows),
        grid_spec=grid_spec,
        out_shape=jax.ShapeDtypeStruct((2, n, out_rows, KV_COLS), BF16),
        compiler_params=_params("arbitrary", "arbitrary"),
        name="compress_paged",
    )(page_table, pool_k, pool_v, new_sub, perm, w1cat, b1_row, w2bd, b2_row)


PAGES_PER_TILE = SEL_TK // PAGE_SIZE


def _nsa_sample_paged_body(pt_ref, pool_k, pool_v, q_ref, kcc_ref, vcc_ref, ksn_ref, vsn_ref,
                           kwc_ref, vwc_ref, kwn_ref, vwn_ref, gates_ref, ovt_ref, zb_ref, o_ref,
                           kbuf, vbuf, ksem, vsem, *, n_pages, n_new, win_buf):
    i = pl.program_id(0)
    n_seq = pl.num_programs(0)
    slot = i % 2
    past_len = n_pages * PAGE_SIZE

    def start(seq, slot_idx):
        _start_pages(pt_ref, seq, pool_k, kbuf, slot_idx, ksem, n_pages)
        _start_pages(pt_ref, seq, pool_v, vbuf, slot_idx, vsem, n_pages)

    @pl.when(i == 0)
    def _():
        start(i, slot)

    @pl.when(i + 1 < n_seq)
    def _():
        start(i + 1, 1 - slot)

    rows = N_HEADS * n_new
    grows = GROUP * n_new
    n_cmp = kcc_ref.shape[2]
    n_selp = ovt_ref.shape[0]
    qb = q_ref[0]
    row = lax.broadcasted_iota(jnp.int32, (rows, 1), 0)
    q_pos = past_len + row % n_new

    s = _dot_nt(qb, kcc_ref[0, 0])
    cmp_end = lax.broadcasted_iota(jnp.int32, (1, n_cmp), 1) * CMP_STRIDE + (CMP_BLOCK - 1)
    p_c = _softmax_rows(s, cmp_end <= q_pos)
    o_c = jnp.dot(p_c.astype(BF16), vcc_ref[0, 0], preferred_element_type=F32)

    parts = []
    for g in range(N_KV):
        acc = p_c[g * grows:g * grows + n_new]
        for r in range(1, GROUP):
            acc = acc + p_c[g * grows + r * n_new:g * grows + (r + 1) * n_new]
        parts.extend([acc] * GROUP)
    p_sum = jnp.concatenate(parts, axis=0)
    imp = _dot_nt(ovt_ref[...], p_sum, precision=lax.Precision.HIGHEST)
    jj = lax.broadcasted_iota(jnp.int32, (n_selp, rows), 0)
    cur = (past_len + lax.broadcasted_iota(jnp.int32, (n_selp, rows), 1) % n_new) // SEL_BLOCK
    forced = (jj == 0) | (jj == cur) | (jj == cur - 1)
    imp = jnp.where(forced, FORCE, imp)
    imp = jnp.where(jj <= cur, imp, NEG)
    sel = jnp.zeros((n_selp, rows), F32)
    n_sel = (past_len + n_new + SEL_BLOCK - 1) // SEL_BLOCK
    for _ in range(min(N_SEL, n_sel)):
        mx = jnp.max(imp, axis=0, keepdims=True)
        first = jnp.min(jnp.where(imp == mx, jj, n_selp), axis=0, keepdims=True)
        pick = jj == first
        sel = jnp.where(pick, 1.0, sel)
        imp = jnp.where(pick, BELOW_NEG, imp)
    qa = jnp.concatenate([qb, jnp.where(sel.T > 0.0, 0.0, NEG).astype(BF16)], axis=1)

    def attend(carry, kt, vt, k0, width, causal):
        m, l, acc = carry
        st = jnp.dot(qa, jnp.concatenate([kt, zb_ref[:, pl.ds(k0, width)]], axis=0), preferred_element_type=F32)
        if causal:
            key = k0 + lax.broadcasted_iota(jnp.int32, (1, width), 1)
            st = jnp.where(key <= q_pos, st, NEG)
        m_new = jnp.maximum(m, jnp.max(st, axis=-1, keepdims=True))
        alpha = jnp.exp(m - m_new)
        p = jnp.exp(st - m_new)
        l = alpha * l + jnp.sum(p, axis=-1, keepdims=True)
        acc = alpha * acc + _dot_nt(p.astype(BF16), vt)
        return m_new, l, acc

    _wait_pages(pool_k, kbuf, slot, ksem, n_pages)
    _wait_pages(pool_v, vbuf, slot, vsem, n_pages)

    def sel_tile(c, carry):
        pages = lambda buf: jnp.concatenate(
            [buf[slot, c * PAGES_PER_TILE + p].astype(BF16) for p in range(PAGES_PER_TILE)], axis=1)
        return attend(carry, pages(kbuf), pages(vbuf), pl.multiple_of(c * SEL_TK, SEL_TK), SEL_TK, False)

    carry = (jnp.full((rows, 1), NEG, F32), jnp.zeros((rows, 1), F32), jnp.zeros((rows, KV_COLS), F32))
    carry = lax.fori_loop(0, n_pages // PAGES_PER_TILE, sel_tile, carry, unroll=2)
    _, l_s, acc_s = attend(carry, ksn_ref[0], vsn_ref[0], past_len, ksn_ref.shape[2], True)
    o_s = acc_s / l_s

    kwt = jnp.concatenate([kwc_ref[0].astype(BF16), kwn_ref[0]], axis=1)
    vwt = jnp.concatenate([vwc_ref[0].astype(BF16), vwn_ref[0]], axis=1)
    sw = jnp.dot(qb, kwt, preferred_element_type=F32)
    kpos = past_len - win_buf + lax.broadcasted_iota(jnp.int32, (1, kwt.shape[1]), 1)
    dist = q_pos - kpos
    p_w = _softmax_rows(sw, (dist >= 0) & (dist < WINDOW) & (kpos >= 0))
    o_w = _dot_nt(p_w.astype(BF16), vwt)

    gt = gates_ref[0]
    o = gt[:, 0:1] * o_c + gt[:, 1:2] * o_s + gt[:, 2:3] * o_w
    diag = lax.broadcasted_iota(jnp.int32, (rows, KV_COLS), 1) // HEAD_DIM == row // grows
    o_ref[0] = jnp.where(diag, o, 0.0)


def _nsa_sample_paged(page_table, pool_k, pool_v, qbd, kvcc, ksn, vsn, kwc, vwc, kwn, vwn, gate_rows, *, n_new, win_buf):
    n, n_pages = page_table.shape
    n_cmp = kvcc.shape[2]
    n_sel = (n_pages * PAGE_SIZE + n_new + SEL_BLOCK - 1) // SEL_BLOCK
    n_selp = (n_sel + LANES - 1) // LANES * LANES
    assert n_pages % PAGES_PER_TILE == 0 and n_pages >= PAGES_PER_TILE
    ovt = jnp.asarray(_overlap_matrix(n_cmp, n_selp).T)
    n_keys = n_pages * PAGE_SIZE + ksn.shape[2]
    zb = jnp.asarray(np.arange(n_selp)[:, None] == np.arange(n_keys)[None, :] // SEL_BLOCK, BF16)
    per = lambda a: pl.BlockSpec((1,) + a.shape[1:], lambda i, pt: (i,) + (0,) * (a.ndim - 1))
    grid_spec = pltpu.PrefetchScalarGridSpec(
        num_scalar_prefetch=1,
        grid=(n,),
        in_specs=[
            pl.BlockSpec(memory_space=pl.ANY), pl.BlockSpec(memory_space=pl.ANY),
            per(qbd),
            pl.BlockSpec((1, 1) + kvcc.shape[2:], lambda i, pt: (0, i, 0, 0)),
            pl.BlockSpec((1, 1) + kvcc.shape[2:], lambda i, pt: (1, i, 0, 0)),
            per(ksn), per(vsn), per(kwc), per(vwc), per(kwn), per(vwn), per(gate_rows),
            pl.BlockSpec(ovt.shape, lambda i, pt: (0, 0)),
            pl.BlockSpec(zb.shape, lambda i, pt: (0, 0)),
        ],
        out_specs=pl.BlockSpec((1, N_HEADS * n_new, KV_COLS), lambda i, pt: (i, 0, 0)),
        scratch_shapes=[pltpu.VMEM((2, n_pages, KV_COLS, PAGE_SIZE), F32),
                        pltpu.VMEM((2, n_pages, KV_COLS, PAGE_SIZE), F32),
                        pltpu.SemaphoreType.DMA((2,)), pltpu.SemaphoreType.DMA((2,))],
    )
    return pl.pallas_call(
        functools.partial(_nsa_sample_paged_body, n_pages=n_pages, n_new=n_new, win_buf=win_buf),
        grid_spec=grid_spec,
        out_shape=jax.ShapeDtypeStruct((n, N_HEADS * n_new, KV_COLS), F32),
        compiler_params=_params("arbitrary"),
        name="nsa_sample_paged",
    )(page_table, pool_k, pool_v, qbd, kvcc, kvcc, ksn, vsn, kwc, vwc, kwn, vwn, gate_rows, ovt, zb)


def _merge_body(x_ref, on_ref, gu_ref, v_ref, sga_ref, sgb_ref, wsg_ref, bsg_ref,
                wpa_ref, wpb_ref, wo_ref, g1_ref, b1_ref, wr_ref, br_ref,
                h_ref, route_ref, *, alpha):
    tm = x_ref.shape[0]
    lane = lax.broadcasted_iota(jnp.int32, (CHUNK, LANES), 1)
    tri = lax.broadcasted_iota(jnp.int32, (CHUNK, CHUNK), 0) >= lax.broadcasted_iota(jnp.int32, (CHUNK, CHUNK), 1)
    zero = jnp.zeros((CHUNK, CHUNK), BF16)
    chunks = []
    for c in range(tm // CHUNK):
        rs = slice(c * CHUNK, (c + 1) * CHUNK)
        cols = []
        for p in range(A_GROUPS // 2):
            cs = slice(p * LANES, (p + 1) * LANES)
            v2 = v_ref[rs, cs].astype(BF16)
            lo = jnp.dot(jnp.where(tri, wsg_ref[0, 2 * p], zero), v2, preferred_element_type=F32)
            hi = jnp.dot(jnp.where(tri, wsg_ref[0, 2 * p + 1], zero), v2, preferred_element_type=F32)
            mixed = jnp.where(lane < A_GDIM, lo, hi) + bsg_ref[0, :, cs]
            cols.append(gu_ref[rs, cs] * mixed)
        chunks.append(jnp.concatenate(cols, axis=1))
    a_out = jnp.concatenate(chunks, axis=0).astype(BF16)
    m = (sga_ref[...] * jnp.dot(a_out, wpa_ref[...], preferred_element_type=F32)
         + sgb_ref[...] * jnp.dot(on_ref[...], wpb_ref[...], preferred_element_type=F32))
    pre = alpha * x_ref[...] + jnp.dot(m.astype(BF16), wo_ref[...], preferred_element_type=F32)
    h = _layer_norm_rows(pre, g1_ref[...], b1_ref[...])
    h_ref[...] = h

    lg = jnp.dot(h, wr_ref[...], preferred_element_type=F32, precision=lax.Precision.HIGHEST) + br_ref[...]
    col = lax.broadcasted_iota(jnp.int32, (tm, LANES), 1)
    is_g = col < N_GROUPS
    gl = jnp.where(is_g, lg, BELOW_NEG)
    gmax = jnp.max(gl, axis=1, keepdims=True)
    grp = jnp.min(jnp.where(gl == gmax, col, LANES), axis=1, keepdims=True)
    p_grp = 1.0 / jnp.sum(jnp.where(is_g, jnp.exp(gl - gmax), 0.0), axis=1, keepdims=True)
    e_idx = col - N_GROUPS
    in_grp = (e_idx >= 0) & (e_idx < N_EXPERTS) & (e_idx // EXPERTS_PER_GROUP == grp)
    el = jnp.where(in_grp, lg, BELOW_NEG)
    m1 = jnp.max(el, axis=1, keepdims=True)
    i1 = jnp.min(jnp.where(el == m1, col, LANES), axis=1, keepdims=True)
    el2 = jnp.where(col == i1, BELOW_NEG, el)
    m2 = jnp.max(el2, axis=1, keepdims=True)
    i2 = jnp.min(jnp.where(el2 == m2, col, LANES), axis=1, keepdims=True)
    e2 = jnp.exp(m2 - m1)
    w1 = p_grp / (1.0 + e2)
    w2 = p_grp * e2 / (1.0 + e2)
    route = jnp.where(col == 0, (i1 - N_GROUPS).astype(F32),
                      jnp.where(col == 1, (i2 - N_GROUPS).astype(F32),
                                jnp.where(col == 2, w1, jnp.where(col == 3, w2, 0.0))))
    route_ref[...] = route


def _merge(x, o_nsa, gu, v, sga, sgb, wsg, bsg, wpa, wpb, wo, g1, b1, wr, br, *, alpha, n_first):
    t = x.shape[0]
    tm = MERGE_TM
    assert t % tm == 0
    row = lambda n: pl.BlockSpec((tm, n), lambda i: (i, 0))
    full = lambda a: pl.BlockSpec(a.shape, lambda i: (0,) * a.ndim)
    pick = lambda a: pl.BlockSpec((1,) + a.shape[1:], lambda i: (jnp.where(i < n_first, 0, 1),) + (0,) * (a.ndim - 1))
    return pl.pallas_call(
        functools.partial(_merge_body, alpha=alpha),
        grid=(t // tm,),
        in_specs=[row(D_MODEL), row(D_MODEL), row(A_HALF), row(A_HALF), row(D_MODEL), row(D_MODEL),
                  pick(wsg), pick(bsg), full(wpa), full(wpb), full(wo), full(g1), full(b1), full(wr), full(br)],
        out_specs=[row(D_MODEL), row(LANES)],
        out_shape=[jax.ShapeDtypeStruct((t, D_MODEL), F32), jax.ShapeDtypeStruct((t, LANES), F32)],
        compiler_params=_params("parallel"),
        name="merge",
    )(x, o_nsa, gu, v, sga, sgb, wsg, bsg, wpa, wpb, wo, g1, b1, wr, br)


def _row_copy(h_ref, tok, xbuf, slot, r, sem):
    return pltpu.make_async_copy(h_ref.at[pl.ds(tok, 1)], xbuf.at[slot, pl.ds(r, 1)], sem.at[slot])


def _experts_body(be_ref, st_ref, nb_ref, h_ref, wg_ref, wu_ref, wd_ref, o_ref, xbuf, sem):
    i = pl.program_id(0)
    bm = xbuf.shape[1]
    slot = i % 2
    n_used = nb_ref[0]

    def start(blk, slot_idx):
        def body(r, c):
            _row_copy(h_ref, st_ref[blk * bm + r], xbuf, slot_idx, r, sem).start()
            return c
        lax.fori_loop(0, bm, body, 0, unroll=8)

    @pl.when((i == 0) & (n_used > 0))
    def _():
        start(i, slot)

    @pl.when(i + 1 < n_used)
    def _():
        start(i + 1, 1 - slot)

    @pl.when(i < n_used)
    def _():
        def body(r, c):
            _row_copy(h_ref, 0, xbuf, slot, r, sem).wait()
            return c
        lax.fori_loop(0, bm, body, 0, unroll=8)
        x = xbuf[slot].astype(BF16)
        gate = jnp.dot(x, wg_ref[0], preferred_element_type=F32)
        up = jnp.dot(x, wu_ref[0], preferred_element_type=F32)
        hid = (jax.nn.silu(gate) * up).astype(BF16)
        o_ref[...] = jnp.dot(hid, wd_ref[0], preferred_element_type=F32)

    @pl.when(i >= n_used)
    def _():
        o_ref[...] = jnp.zeros(o_ref.shape, F32)


def _experts(blk_exp, slot_tok, n_used, h, wg, wu, wd):
    n_slots = slot_tok.shape[0]
    bm = MOE_BM
    grid_spec = pltpu.PrefetchScalarGridSpec(
        num_scalar_prefetch=3,
        grid=(n_slots // bm,),
        in_specs=[
            pl.BlockSpec(memory_space=pl.ANY),
            pl.BlockSpec((1, D_MODEL, D_EXPERT), lambda i, be, st, nb: (be[i], 0, 0)),
            pl.BlockSpec((1, D_MODEL, D_EXPERT), lambda i, be, st, nb: (be[i], 0, 0)),
            pl.BlockSpec((1, D_EXPERT, D_MODEL), lambda i, be, st, nb: (be[i], 0, 0)),
        ],
        out_specs=pl.BlockSpec((bm, D_MODEL), lambda i, be, st, nb: (i, 0)),
        scratch_shapes=[pltpu.VMEM((2, bm, D_MODEL), F32), pltpu.SemaphoreType.DMA((2,))],
    )
    return pl.pallas_call(
        _experts_body,
        grid_spec=grid_spec,
        out_shape=jax.ShapeDtypeStruct((n_slots, D_MODEL), F32),
        compiler_params=_params("arbitrary"),
        name="experts",
    )(blk_exp, slot_tok, n_used, h, wg, wu, wd)


def _final_body(h_ref, y_ref, g_ref, b_ref, o_ref, *, alpha):
    o_ref[...] = _layer_norm_rows(alpha * h_ref[...] + y_ref[...], g_ref[...], b_ref[...])


def _final_norm(h, y, g, b, *, alpha):
    t = h.shape[0]
    tm = MERGE_TM
    row = pl.BlockSpec((tm, D_MODEL), lambda i: (i, 0))
    full = lambda a: pl.BlockSpec(a.shape, lambda i: (0,) * a.ndim)
    return pl.pallas_call(
        functools.partial(_final_body, alpha=alpha),
        grid=(t // tm,),
        in_specs=[row, row, full(g), full(b)],
        out_specs=row,
        out_shape=jax.ShapeDtypeStruct((t, D_MODEL), F32),
        compiler_params=_params("parallel"),
        name="final_norm",
    )(h, y, g, b)


def _moe_dispatch(route):
    t = route.shape[0]
    n_assign = t * TOP_K
    flat_e = route[:, 0:TOP_K].astype(jnp.int32).reshape(-1)
    onehot = (flat_e[:, None] == jnp.arange(N_EXPERTS, dtype=jnp.int32)[None, :]).astype(jnp.int32)
    csum = jnp.cumsum(onehot, axis=0)
    counts = csum[-1]
    rank = jnp.sum(onehot * csum, axis=1) - 1
    padded = (counts + MOE_BM - 1) // MOE_BM * MOE_BM
    pad_end = jnp.cumsum(padded)
    pad_start = pad_end - padded
    start = jnp.cumsum(counts) - counts
    dest = (jnp.sum(onehot * pad_start[None, :], axis=1) + rank).reshape(t, TOP_K)
    n_blocks = -(-n_assign // MOE_BM) + N_EXPERTS
    blk_start = jnp.arange(n_blocks, dtype=jnp.int32) * MOE_BM
    blk_exp = jnp.minimum(jnp.sum(pad_end[None, :] <= blk_start[:, None], axis=1), N_EXPERTS - 1).astype(jnp.int32)
    order = jnp.argsort(flat_e)
    e_slot = jnp.repeat(blk_exp, MOE_BM)
    off = jnp.arange(n_blocks * MOE_BM, dtype=jnp.int32) - pad_start[e_slot]
    src = jnp.clip(start[e_slot] + off, 0, n_assign - 1)
    slot_tok = jnp.where(off < counts[e_slot], order[src].astype(jnp.int32) // TOP_K, 0)
    n_used = (pad_end[-1:] // MOE_BM).astype(jnp.int32)
    return slot_tok, dest, blk_exp, n_used


def _nsa_sample_glue(page_table, cmp_k, cmp_v, sel_k, sel_v, win_k, win_v, q, kc, vc, ksb, vsb, kwb, vwb, gates, cw, *, n_new):
    n, n_pages = page_table.shape
    ts = n * n_new
    win_buf = win_k.shape[1]
    new = lambda a: a.reshape(n, n_new, KV_COLS)
    cmp_out = (n_pages * (PAGE_SIZE // CMP_STRIDE) + 16 + LANES - 1) // LANES * LANES
    kvcc = _compress_paged(page_table, _page_view(cmp_k), _page_view(cmp_v), jnp.stack([new(kc), new(vc)]), cw, out_rows=cmp_out)
    new_t = lambda a: jnp.pad(new(a).transpose(0, 2, 1), ((0, 0), (0, 0), (0, LANES - n_new)))
    q_s = q.reshape(n, n_new, N_KV, GROUP, HEAD_DIM)
    qbd = jnp.einsum('ntgrd,gj->ngrtjd', q_s, jnp.eye(N_KV, dtype=BF16)).reshape(n, N_HEADS * n_new, KV_COLS)
    g_s = gates.reshape(n, n_new, N_KV, LANES)[..., :3 * GROUP].reshape(n, n_new, N_KV, 3, GROUP)
    gate_rows = jnp.pad(g_s.transpose(0, 2, 4, 1, 3).reshape(n, N_HEADS * n_new, 3), ((0, 0), (0, 0), (0, LANES - 3)))
    od = _nsa_sample_paged(page_table, _page_view(sel_k), _page_view(sel_v), qbd, kvcc,
                           new_t(ksb), new_t(vsb), _page_view(win_k), _page_view(win_v),
                           new_t(kwb), new_t(vwb), gate_rows, n_new=n_new, win_buf=win_buf)
    o_s = jnp.einsum('ngrtgd->ntgrd', od.reshape(n, N_KV, GROUP, n_new, N_KV, HEAD_DIM))
    return o_s.reshape(ts, N_HEADS * HEAD_DIM).astype(BF16)


def _sgu_weights(sgu_w, sgu_b, n_new):
    reps = CHUNK // n_new
    w_short = jnp.einsum('ab,gts->gatbs', jnp.eye(reps, dtype=F32), sgu_w[:, :n_new, :n_new]).reshape(A_GROUPS, CHUNK, CHUNK)
    wsg = jnp.stack([sgu_w[:, :CHUNK, :CHUNK], w_short]).astype(BF16)
    b_full = jnp.repeat(sgu_b[:, :CHUNK].T, A_GDIM, axis=1)
    b_short = jnp.tile(jnp.repeat(sgu_b[:, :n_new].T, A_GDIM, axis=1), (reps, 1))
    return wsg, jnp.stack([b_full, b_short])


def kernel(x_prompt, x_sample, cache_cmp_k, cache_cmp_v, cache_sel_k, cache_sel_v, cache_win_k, cache_win_v, page_table, w_in, cmp_pe, cmp_w1, cmp_b1, cmp_w2, cmp_b2, sgu_ln_g, sgu_ln_b, sgu_w, sgu_b, w_proj_a, w_proj_b, w_out, ln1_g, ln1_b, router_group_w, router_group_b, router_expert_w, router_expert_b, exp_w_gate, exp_w_up, exp_w_down, ln2_g, ln2_b):
    depth = w_in.shape[0]
    alpha = (2.0 * depth) ** 0.25
    b, t, _ = x_prompt.shape
    n, n_new, _ = x_sample.shape
    tp, ts = b * t, n * n_new
    past_len = page_table.shape[1] * PAGE_SIZE
    win_buf = cache_win_k.shape[2]
    assert CHUNK % n_new == 0 and tp % MERGE_TM == 0 and ts % MERGE_TM == 0

    x_p, x_s = x_prompt, x_sample
    states = []
    for l in range(depth):
        x_all = jnp.concatenate([x_p.reshape(tp, D_MODEL), x_s.reshape(ts, D_MODEL)], axis=0)
        (q, kc, vc, ks, vs, kw, vw, ksb, vsb, kwb, vwb, gu, v, sga, sgb, gates) = _in_proj(
            x_all, _pack_w_in(w_in[l]), sgu_ln_g[l][None], sgu_ln_b[l][None])
        cw = _compress_weights(cmp_pe[l], cmp_w1[l], cmp_b1[l], cmp_w2[l], cmp_b2[l])

        sub_w = CMP_STRIDE * KV_COLS
        kcc, vcc = _compress(kc[:tp].reshape(b, t // CMP_STRIDE, sub_w), vc[:tp].reshape(b, t // CMP_STRIDE, sub_w),
                             cw, tile=128, out_rows=t // CMP_STRIDE)
        seq = lambda a: a[:tp].reshape(b, t, -1)
        o_p = _nsa_prompt_glue(seq(q), kcc, vcc, seq(ksb), seq(vsb), seq(kwb), seq(vwb), seq(gates))

        tail = lambda a: a[tp:]
        o_s = _nsa_sample_glue(page_table, cache_cmp_k[l], cache_cmp_v[l], cache_sel_k[l], cache_sel_v[l],
                               cache_win_k[l], cache_win_v[l], tail(q), tail(kc), tail(vc), tail(ksb), tail(vsb),
                               tail(kwb), tail(vwb), tail(gates), cw, n_new=n_new)
        o_all = jnp.concatenate([o_p.reshape(tp, -1), o_s], axis=0)

        wsg, bsg = _sgu_weights(sgu_w[l], sgu_b[l], n_new)
        wr = jnp.pad(jnp.concatenate([router_group_w[l], router_expert_w[l]], axis=1), ((0, 0), (0, LANES - N_GROUPS - N_EXPERTS)))
        br = jnp.pad(jnp.concatenate([router_group_b[l], router_expert_b[l]]), (0, LANES - N_GROUPS - N_EXPERTS))[None]
        h, route = _merge(x_all, o_all, gu, v, sga, sgb, wsg, bsg,
                              w_proj_a[l].astype(BF16), w_proj_b[l].astype(BF16), w_out[l].astype(BF16),
                              ln1_g[l][None], ln1_b[l][None], wr, br, alpha=alpha, n_first=tp // MERGE_TM)

        slot_tok, dest, blk_exp, n_used = _moe_dispatch(route)
        out = _experts(blk_exp, slot_tok, n_used, h,
                       exp_w_gate[l].astype(BF16), exp_w_up[l].astype(BF16), exp_w_down[l].astype(BF16))
        y_moe = route[:, 2:3] * out[dest[:, 0]] + route[:, 3:4] * out[dest[:, 1]]
        y_all = _final_norm(h, y_moe, ln2_g[l][None], ln2_b[l][None], alpha=alpha)

        kv5 = lambda a, lo, hi, bb: a[lo:hi].reshape(bb, -1, N_KV, HEAD_DIM)
        keep_p = min(WINDOW, t)
        prompt_state = tuple(kv5(a, 0, tp, b) for a in (kc, vc, ks, vs)) + tuple(
            kv5(a, 0, tp, b)[:, t - keep_p:] for a in (kw, vw))
        keep_s = min(WINDOW, win_buf + n_new)
        win_all = lambda cache, a: jnp.concatenate([cache[l], kv5(a, tp, tp + ts, n)], axis=1)[:, win_buf + n_new - keep_s:]
        sample_state = tuple(kv5(a, tp, tp + ts, n) for a in (kc, vc, ks, vs)) + (
            win_all(cache_win_k, kw), win_all(cache_win_v, vw), v[tp:].reshape(n, n_new, A_HALF))
        states.append(prompt_state + sample_state)
        x_p, x_s = y_all[:tp].reshape(b, t, D_MODEL), y_all[tp:].reshape(n, n_new, D_MODEL)
    return (x_p, x_s) + tuple(jnp.stack(z) for z in zip(*states))
```

```python
import functools

import numpy as np
import jax
import jax.numpy as jnp
from jax import lax
from jax.experimental import pallas as pl
from jax.experimental.pallas import tpu as pltpu

F32 = jnp.float32
BF16 = jnp.bfloat16

D_MODEL = 1024
N_HEADS = 16
HEAD_DIM = 64
N_KV = 4
GROUP = N_HEADS // N_KV
KV_COLS = N_KV * HEAD_DIM
CMP_STRIDE = 16
CMP_BLOCK = 32
CMP_HIDDEN = 64
SEL_BLOCK = 64
N_SEL = 16
WINDOW = 512
Q_BLOCK = 128
PAGE_SIZE = 128
CHUNK = 128
A_GROUPS = 8
A_HALF = D_MODEL // 2
A_GDIM = A_HALF // A_GROUPS
N_GROUPS = 4
EXPERTS_PER_GROUP = 8
N_EXPERTS = N_GROUPS * EXPERTS_PER_GROUP
TOP_K = 2
D_EXPERT = D_MODEL // 2
LN_EPS = 1e-5
NEG = -1e30
FORCE = 1e9
BELOW_NEG = -3e38

LANES = 128
VMEM_LIMIT = 56 * 1024 * 1024

INPROJ_TM = 256
MERGE_TM = 512
MOE_BM = 256
SEL_TK = 512
SEL_UNROLL = 4


def _gelu(x):
    return jax.nn.gelu(x, approximate=True)


def _layer_norm_rows(x, g, b):
    mu = jnp.mean(x, axis=-1, keepdims=True)
    xc = x - mu
    var = jnp.mean(xc * xc, axis=-1, keepdims=True)
    return xc * lax.rsqrt(var + LN_EPS) * g + b


def _params(*sem):
    return pltpu.CompilerParams(dimension_semantics=tuple(sem), vmem_limit_bytes=VMEM_LIMIT)


_C_Q = 0
_C_KV = D_MODEL
_C_AU = _C_KV + 6 * KV_COLS
_C_AV = _C_AU + A_HALF
_C_GA = _C_AV + A_HALF
_C_GB = _C_GA + D_MODEL
_C_GATE = _C_GB + D_MODEL
_C_END = _C_GATE + N_KV * LANES


def _pack_w_in(w_in):
    q_cols = N_HEADS * HEAD_DIM
    o_gate = q_cols + 6 * KV_COLS
    o_au = o_gate + 3 * N_HEADS
    src = np.zeros((N_KV * LANES,), np.int32)
    valid = np.zeros((N_KV * LANES,), np.float32)
    for g in range(N_KV):
        for c in range(3):
            for r in range(GROUP):
                src[g * LANES + c * GROUP + r] = o_gate + (g * GROUP + r) * 3 + c
                valid[g * LANES + c * GROUP + r] = 1.0
    w_gate = w_in[:, src] * jnp.asarray(valid)
    packed = jnp.concatenate([w_in[:, :o_gate], w_in[:, o_au:], w_gate], axis=1)
    return packed.astype(BF16)


def _inproj_body(x_ref, w_ref, lng_ref, lnb_ref,
                 q_ref, kc_ref, vc_ref, ks_ref, vs_ref, kw_ref, vw_ref,
                 ksb_ref, vsb_ref, kwb_ref, vwb_ref,
                 gu_ref, v_ref, sga_ref, sgb_ref, gates_ref):
    x = x_ref[...].astype(BF16)

    def mm(c0, n):
        return jnp.dot(x, w_ref[:, c0:c0 + n], preferred_element_type=F32)

    q_ref[...] = (mm(_C_Q, D_MODEL) * (HEAD_DIM ** -0.5)).astype(BF16)
    f32_refs = (kc_ref, vc_ref, ks_ref, vs_ref, kw_ref, vw_ref)
    bf_refs = (None, None, ksb_ref, vsb_ref, kwb_ref, vwb_ref)
    for i in range(6):
        y = mm(_C_KV + i * KV_COLS, KV_COLS)
        f32_refs[i][...] = y
        if bf_refs[i] is not None:
            bf_refs[i][...] = y.astype(BF16)
    gu_ref[...] = _gelu(mm(_C_AU, A_HALF))
    v_ref[...] = _layer_norm_rows(_gelu(mm(_C_AV, A_HALF)), lng_ref[...], lnb_ref[...])
    sga_ref[...] = jax.nn.sigmoid(mm(_C_GA, D_MODEL))
    sgb_ref[...] = jax.nn.sigmoid(mm(_C_GB, D_MODEL))
    gates_ref[...] = jax.nn.sigmoid(mm(_C_GATE, N_KV * LANES))


def _in_proj(x, w_packed, ln_g, ln_b):
    t = x.shape[0]
    tm = INPROJ_TM
    assert t % tm == 0
    row = lambda n: pl.BlockSpec((tm, n), lambda i: (i, 0))
    full = lambda a: pl.BlockSpec(a.shape, lambda i: (0,) * a.ndim)
    out_shapes = (
        [jax.ShapeDtypeStruct((t, D_MODEL), BF16)]
        + [jax.ShapeDtypeStruct((t, KV_COLS), F32)] * 6
        + [jax.ShapeDtypeStruct((t, KV_COLS), BF16)] * 4
        + [jax.ShapeDtypeStruct((t, A_HALF), F32)] * 2
        + [jax.ShapeDtypeStruct((t, D_MODEL), F32)] * 2
        + [jax.ShapeDtypeStruct((t, N_KV * LANES), F32)]
    )
    out_specs = [row(s.shape[1]) for s in out_shapes]
    return pl.pallas_call(
        _inproj_body,
        grid=(t // tm,),
        in_specs=[row(D_MODEL), full(w_packed), full(ln_g), full(ln_b)],
        out_specs=out_specs,
        out_shape=out_shapes,
        compiler_params=_params("parallel"),
        name="in_proj",
    )(x, w_packed, ln_g, ln_b)


def _compress_weights(pe, w1, b1, w2, b2):
    eye = jnp.eye(N_KV, dtype=F32)
    w1r = w1.reshape(2, 2, CMP_STRIDE, HEAD_DIM, CMP_HIDDEN)
    w1bd = jnp.einsum('kfsdh,gj->kfsgdjh', w1r, eye).reshape(
        2, 2, CMP_STRIDE * KV_COLS, N_KV * CMP_HIDDEN).astype(BF16)
    per = pe.reshape(2, 2, CMP_STRIDE, 1, HEAD_DIM)
    pe_row = jnp.broadcast_to(per, (2, 2, CMP_STRIDE, N_KV, HEAD_DIM)).reshape(2, 2, 1, CMP_STRIDE * KV_COLS)
    b1_row = jnp.tile(b1, (1, N_KV)).reshape(2, 1, N_KV * CMP_HIDDEN)
    w2bd = jnp.einsum('khd,gj->kghjd', w2, eye).reshape(2, N_KV * CMP_HIDDEN, KV_COLS).astype(BF16)
    b2_row = jnp.tile(b2, (1, N_KV)).reshape(2, 1, KV_COLS)
    w1pair = jnp.einsum('kfsdh,ab->ksadfbh', w1r, jnp.eye(2, dtype=F32)).reshape(
        2, CMP_STRIDE * KV_COLS // 2, 2 * 2 * CMP_HIDDEN).astype(BF16)
    return pe_row, w1bd, b1_row, w2bd, b2_row, w1pair


def _compress_body(xk_ref, xv_ref, pe_ref, w1_ref, b1_ref, w2_ref, b2_ref, ok_ref, ov_ref,
                   first_ref, second_ref, *, rows, tile, out_rows):
    j = pl.program_id(1)
    r0 = pl.multiple_of(j * tile, 8)
    for kv, x_ref in enumerate((xk_ref, xv_ref)):
        x = x_ref[0]
        first_ref[kv, pl.ds(r0, tile), :] = jnp.dot(
            (x + pe_ref[kv, 0]).astype(BF16), w1_ref[kv, 0], preferred_element_type=F32)
        second_ref[kv, pl.ds(r0, tile), :] = jnp.dot(
            (x + pe_ref[kv, 1]).astype(BF16), w1_ref[kv, 1], preferred_element_type=F32)

    @pl.when(j == pl.num_programs(1) - 1)
    def _():
        for kv, o_ref in enumerate((ok_ref, ov_ref)):
            nxt = pltpu.roll(second_ref[kv], rows - 1, axis=0)
            hid = _gelu(first_ref[kv] + nxt + b1_ref[kv])
            out = jnp.dot(hid.astype(BF16), w2_ref[kv], preferred_element_type=F32) + b2_ref[kv]
            o_ref[0, 0:rows, :] = out.astype(BF16)
            if out_rows > rows:
                o_ref[0, rows:out_rows, :] = jnp.zeros((out_rows - rows, KV_COLS), BF16)


def _compress(xk, xv, cw, *, tile, out_rows):
    pe_row, w1bd, b1_row, w2bd, b2_row, _ = cw
    n, rows, width = xk.shape
    assert rows % tile == 0 and tile % 8 == 0
    xspec = pl.BlockSpec((1, tile, width), lambda i, j: (i, j, 0))
    full = lambda a: pl.BlockSpec(a.shape, lambda i, j: (0,) * a.ndim)
    ospec = pl.BlockSpec((1, out_rows, KV_COLS), lambda i, j: (i, 0, 0))
    oshape = jax.ShapeDtypeStruct((n, out_rows, KV_COLS), BF16)
    return pl.pallas_call(
        functools.partial(_compress_body, rows=rows, tile=tile, out_rows=out_rows),
        grid=(n, rows // tile),
        in_specs=[xspec, xspec, full(pe_row), full(w1bd), full(b1_row), full(w2bd), full(b2_row)],
        out_specs=[ospec, ospec],
        out_shape=[oshape, oshape],
        scratch_shapes=[pltpu.VMEM((2, rows, N_KV * CMP_HIDDEN), F32),
                        pltpu.VMEM((2, rows, N_KV * CMP_HIDDEN), F32)],
        compiler_params=_params("parallel", "arbitrary"),
        name="compress",
    )(xk, xv, pe_row, w1bd, b1_row, w2bd, b2_row)


def _softmax_rows(s, mask):
    s = jnp.where(mask, s, NEG)
    e = jnp.exp(s - jnp.max(s, axis=-1, keepdims=True)) * mask.astype(F32)
    return e / jnp.maximum(jnp.sum(e, axis=-1, keepdims=True), 1e-30)


def _dot_nt(a, b, **kw):
    return lax.dot_general(a, b, (((1,), (1,)), ((), ())), preferred_element_type=F32, **kw)


def _overlap_matrix(n_cmp_rows, n_sel_cols):
    c_start = np.arange(n_cmp_rows)[:, None] * CMP_STRIDE
    s_start = np.arange(n_sel_cols)[None, :] * SEL_BLOCK
    return ((c_start < s_start + SEL_BLOCK) & (c_start + CMP_BLOCK > s_start)).astype(np.float32)


def _nsa_prompt_body(q_ref, kcc_ref, vcc_ref, ks_ref, vs_ref, kw_ref, vw_ref, gates_ref, ovt_ref, wb_ref, o_ref):
    i = pl.program_id(2)
    tq = Q_BLOCK
    rows = GROUP * tq
    n_cmp = kcc_ref.shape[2]
    n_sel = ovt_ref.shape[0]
    q2 = q_ref[0]
    qs = jnp.concatenate([q2[:, r * HEAD_DIM:(r + 1) * HEAD_DIM] for r in range(GROUP)], axis=0)
    tok = lax.broadcasted_iota(jnp.int32, (rows, 1), 0) % tq
    q_pos = i * tq + tok

    cmp_end = lax.broadcasted_iota(jnp.int32, (1, n_cmp), 1) * CMP_STRIDE + (CMP_BLOCK - 1)
    s = jnp.where(cmp_end <= q_pos, _dot_nt(qs, kcc_ref[0, 0]), NEG)
    e_c = jnp.exp(s - jnp.max(s, axis=-1, keepdims=True))
    p_c = e_c * jnp.where(q_pos >= CMP_BLOCK - 1, 1.0 / jnp.sum(e_c, axis=-1, keepdims=True), 0.0)
    o_c = jnp.dot(p_c.astype(BF16), vcc_ref[0, 0], preferred_element_type=F32)

    wk = WINDOW + tq
    w0 = pl.multiple_of(i * tq, tq)
    kpos = i * tq - WINDOW + lax.broadcasted_iota(jnp.int32, (1, wk), 1)
    sw = _dot_nt(qs, kw_ref[0, 0, pl.ds(w0, wk), :]) + wb_ref[...] + jnp.where(kpos >= 0, 0.0, NEG)
    e_w = jnp.exp(sw - jnp.max(sw, axis=-1, keepdims=True))
    p_w = e_w * (1.0 / jnp.sum(e_w, axis=-1, keepdims=True))
    o_w = jnp.dot(p_w.astype(BF16), vw_ref[0, 0, pl.ds(w0, wk), :], preferred_element_type=F32)

    gt = gates_ref[0]
    gcol = lambda c: jnp.concatenate([gt[:, c * GROUP + r:c * GROUP + r + 1] for r in range(GROUP)], axis=0)
    o_cw = gcol(0) * o_c + gcol(2) * o_w

    p_sum = p_c[0:tq]
    for r in range(1, GROUP):
        p_sum = p_sum + p_c[r * tq:(r + 1) * tq]
    imp = _dot_nt(ovt_ref[...], p_sum, precision=lax.Precision.HIGHEST)
    jj = lax.broadcasted_iota(jnp.int32, (n_sel, tq), 0)
    cur = (i * tq + lax.broadcasted_iota(jnp.int32, (n_sel, tq), 1)) // SEL_BLOCK
    forced = (jj == 0) | (jj == cur) | (jj == cur - 1)
    imp = jnp.where(forced, FORCE, imp)
    imp = jnp.where(jj <= cur, imp, NEG)
    sel = jnp.zeros((n_sel, tq), F32)
    for _ in range(min(N_SEL, n_sel)):
        mx = jnp.max(imp, axis=0, keepdims=True)
        first = jnp.min(jnp.where(imp == mx, jj, n_sel), axis=0, keepdims=True)
        pick = jj == first
        sel = jnp.where(pick, 1.0, sel)
        imp = jnp.where(pick, BELOW_NEG, imp)
    nb = jnp.where(sel.T > 0.0, 0.0, NEG).astype(BF16)
    qa = jnp.concatenate([jnp.concatenate([nb] * GROUP, axis=0), qs], axis=1)

    def sel_tile(c, carry, causal):
        m, l, acc = carry
        k0 = pl.multiple_of(c * SEL_TK, SEL_TK)
        st = _dot_nt(qa, ks_ref[0, 0, pl.ds(k0, SEL_TK), :])
        if causal:
            key = k0 + lax.broadcasted_iota(jnp.int32, (1, SEL_TK), 1)
            st = jnp.where(key <= q_pos, st, NEG)
        m_new = jnp.maximum(m, jnp.max(st, axis=-1, keepdims=True))
        alpha = jnp.exp(m - m_new)
        p = jnp.exp(st - m_new)
        l = alpha * l + jnp.sum(p, axis=-1, keepdims=True)
        acc = alpha * acc + jnp.dot(p.astype(BF16), vs_ref[0, 0, pl.ds(k0, SEL_TK), :], preferred_element_type=F32)
        return m_new, l, acc

    def tile_group(gi, carry):
        for u in range(SEL_UNROLL):
            carry = sel_tile(SEL_UNROLL * gi + u, carry, False)
        return carry

    carry = (jnp.full((rows, 1), NEG, F32), jnp.zeros((rows, 1), F32), jnp.zeros((rows, HEAD_DIM), F32))
    n_full = (i * tq) // SEL_TK
    n_groups = n_full // SEL_UNROLL
    carry = sel_tile(n_full, carry, True)
    carry = lax.fori_loop(0, n_groups, tile_group, carry)
    _, l_s, acc_s = lax.fori_loop(n_groups * SEL_UNROLL, n_full, functools.partial(sel_tile, causal=False), carry)
    o_s = acc_s / l_s

    o = o_cw + gcol(1) * o_s
    for r in range(GROUP):
        o_ref[0, :, r * HEAD_DIM:(r + 1) * HEAD_DIM] = o[r * tq:(r + 1) * tq].astype(BF16)


def _nsa_prompt(q, kcc, vcc, ks, vs, kw, vw, gates):
    b, t, _ = q.shape
    n_cmp = kcc.shape[2]
    n_sel = t // SEL_BLOCK
    assert t % SEL_TK == 0 and t % Q_BLOCK == 0 and ks.shape[3] == n_sel + HEAD_DIM
    ovt = jnp.asarray(_overlap_matrix(n_cmp, n_sel).T)
    tt = np.arange(GROUP * Q_BLOCK)[:, None] % Q_BLOCK
    col = np.arange(WINDOW + Q_BLOCK)[None, :]
    wb = jnp.asarray(np.where((col > tt) & (col <= tt + WINDOW), 0.0, NEG).astype(np.float32))
    per_bg = lambda a: pl.BlockSpec((1, 1) + a.shape[2:], lambda bi, g, i: (bi, g, 0, 0))
    return pl.pallas_call(
        _nsa_prompt_body,
        grid=(b, N_KV, t // Q_BLOCK),
        in_specs=[
            pl.BlockSpec((1, Q_BLOCK, GROUP * HEAD_DIM), lambda bi, g, i: (bi, i, g)),
            per_bg(kcc), per_bg(vcc), per_bg(ks), per_bg(vs), per_bg(kw), per_bg(vw),
            pl.BlockSpec((1, Q_BLOCK, LANES), lambda bi, g, i: (bi, i, g)),
            pl.BlockSpec(ovt.shape, lambda bi, g, i: (0, 0)),
            pl.BlockSpec(wb.shape, lambda bi, g, i: (0, 0)),
        ],
        out_specs=pl.BlockSpec((1, Q_BLOCK, GROUP * HEAD_DIM), lambda bi, g, i: (bi, i, g)),
        out_shape=jax.ShapeDtypeStruct((b, t, N_HEADS * HEAD_DIM), BF16),
        compiler_params=_params("parallel", "parallel", "arbitrary"),
        name="nsa_prompt",
    )(q, kcc, vcc, ks, vs, kw, vw, gates, ovt, wb)


def _to_heads(a):
    b, l, _ = a.shape
    return a.reshape(b, l, N_KV, HEAD_DIM).transpose(0, 2, 1, 3)


def _nsa_prompt_glue(q, kcc, vcc, ks, vs, kw, vw, gates):
    pad = lambda a: jnp.pad(a, ((0, 0), (WINDOW, 0), (0, 0)))
    b, t, _ = ks.shape
    n_sel = t // SEL_BLOCK
    onehot = (jnp.arange(t)[:, None] // SEL_BLOCK == jnp.arange(n_sel)[None, :]).astype(BF16)
    ks_aug = jnp.concatenate([jnp.broadcast_to(onehot, (b, N_KV, t, n_sel)), _to_heads(ks)], axis=3)
    return _nsa_prompt(q, _to_heads(kcc), _to_heads(vcc), ks_aug, _to_heads(vs),
                       _to_heads(pad(kw)), _to_heads(pad(vw)), gates)


def _page_view(pool):
    return jnp.transpose(pool, (0, 2, 3, 1)).reshape(pool.shape[0], KV_COLS, pool.shape[1])


def _page_copy(pool_ref, page, buf_ref, slot, j, sem_ref):
    return pltpu.make_async_copy(pool_ref.at[page], buf_ref.at[slot, j], sem_ref.at[slot])


def _start_pages(pt_ref, seq, pool_ref, buf_ref, slot, sem_ref, n_pages):
    def body(j, c):
        _page_copy(pool_ref, pt_ref[seq, j], buf_ref, slot, j, sem_ref).start()
        return c
    lax.fori_loop(0, n_pages, body, 0)


def _wait_pages(pool_ref, buf_ref, slot, sem_ref, n_pages):
    def body(j, c):
        _page_copy(pool_ref, 0, buf_ref, slot, j, sem_ref).wait()
        return c
    lax.fori_loop(0, n_pages, body, 0)


def _compress_paged_body(pt_ref, pool_k, pool_v, new_ref, perm_ref, w1_ref, b1_ref, w2_ref, b2_ref,
                         o_ref, pbuf, xsub, sem, *, n_pages, out_rows):
    kv = pl.program_id(0)
    i = pl.program_id(1)
    n_seq = pl.num_programs(1)
    step = kv * n_seq + i
    slot = step % 2

    def start(kv_idx, seq, slot_idx):
        @pl.when(kv_idx == 0)
        def _():
            _start_pages(pt_ref, seq, pool_k, pbuf, slot_idx, sem, n_pages)

        @pl.when(kv_idx == 1)
        def _():
            _start_pages(pt_ref, seq, pool_v, pbuf, slot_idx, sem, n_pages)

    @pl.when(step == 0)
    def _():
        start(kv, i, slot)

    @pl.when(i + 1 < n_seq)
    def _():
        start(kv, i + 1, 1 - slot)

    @pl.when((i + 1 == n_seq) & (kv == 0))
    def _():
        start(kv + 1, 0, 1 - slot)

    _wait_pages(pool_k, pbuf, slot, sem, n_pages)

    def pair(p, c):
        xt = jnp.concatenate([pbuf[slot, 2 * p].astype(BF16), pbuf[slot, 2 * p + 1].astype(BF16)], axis=1)
        y = _dot_nt(perm_ref[...], xt)
        r0 = pl.multiple_of(p * 16, 16)
        for s in range(CMP_STRIDE):
            xsub[pl.ds(r0, 16), s * KV_COLS:(s + 1) * KV_COLS] = y[s * 16:(s + 1) * 16].astype(BF16)
        return c
    lax.fori_loop(0, n_pages // 2, pair, 0, unroll=4)
    past_sub = n_pages * (PAGE_SIZE // CMP_STRIDE)
    rows = past_sub + 16
    xsub[past_sub:rows, :] = new_ref[0, 0]

    xs = xsub[...]
    half = KV_COLS // 2
    yp = []
    for p in range(2):
        xp = jnp.concatenate(
            [xs[:, s * KV_COLS + p * half:s * KV_COLS + (p + 1) * half] for s in range(CMP_STRIDE)], axis=1)
        yp.append(jnp.dot(xp, w1_ref[0], preferred_element_type=F32))
    y = jnp.concatenate([yp[0][:, :half], yp[1][:, :half], yp[0][:, half:], yp[1][:, half:]], axis=1)
    h4 = N_KV * CMP_HIDDEN
    pe_first = y[past_sub + 4:past_sub + 5, :h4] + y[past_sub + 5:past_sub + 6, :h4]
    pe_second = y[past_sub + 6:past_sub + 7, h4:] + y[past_sub + 7:past_sub + 8, h4:]
    first = y[:, :h4] + pe_first
    nxt = pltpu.roll(y[:, h4:] + pe_second, rows - 1, axis=0)
    hid = _gelu(first + nxt + b1_ref[0])
    out = jnp.dot(hid.astype(BF16), w2_ref[0], preferred_element_type=F32) + b2_ref[0]
    o_ref[0, 0, 0:rows, :] = out.astype(BF16)
    o_ref[0, 0, rows:out_rows, :] = jnp.zeros((out_rows - rows, KV_COLS), BF16)


def _compress_paged(page_table, pool_k, pool_v, new_kv, cw, *, out_rows):
    pe_row, w1bd, b1_row, w2bd, b2_row, w1pair = cw
    n, n_pages = page_table.shape
    n_new = new_kv.shape[2]
    assert n_pages % 2 == 0 and n_new <= CMP_STRIDE
    past_sub = n_pages * (PAGE_SIZE // CMP_STRIDE)
    rows = past_sub + 16
    sub_w = CMP_STRIDE * KV_COLS
    pe = pe_row.reshape(2, 2, sub_w)
    pe_hi = pe.astype(BF16)
    pe_lo = (pe - pe_hi.astype(F32)).astype(BF16)
    pe_rows = jnp.stack([pe_hi[:, 0], pe_lo[:, 0], pe_hi[:, 1], pe_lo[:, 1]], axis=1)
    new_row = jnp.pad(new_kv.reshape(2, n, 1, n_new * KV_COLS), ((0, 0), (0, 0), (0, 3), (0, sub_w - n_new * KV_COLS)))
    new_sub = jnp.concatenate(
        [new_row.astype(BF16), jnp.broadcast_to(pe_rows[:, None], (2, n, 4, sub_w)),
         jnp.zeros((2, n, 8, sub_w), BF16)], axis=2)
    w1cat = w1pair
    m = np.arange(256)
    perm = np.zeros((256, 256), np.float32)
    perm[m, (m % 16) * 16 + m // 16] = 1.0
    perm = jnp.asarray(perm, BF16)
    grid_spec = pltpu.PrefetchScalarGridSpec(
        num_scalar_prefetch=1,
        grid=(2, n),
        in_specs=[
            pl.BlockSpec(memory_space=pl.ANY), pl.BlockSpec(memory_space=pl.ANY),
            pl.BlockSpec((1, 1, 16, sub_w), lambda kv, i, pt: (kv, i, 0, 0)),
            pl.BlockSpec(perm.shape, lambda kv, i, pt: (0, 0)),
            pl.BlockSpec((1,) + w1cat.shape[1:], lambda kv, i, pt: (kv, 0, 0)),
            pl.BlockSpec((1,) + b1_row.shape[1:], lambda kv, i, pt: (kv, 0, 0)),
            pl.BlockSpec((1,) + w2bd.shape[1:], lambda kv, i, pt: (kv, 0, 0)),
            pl.BlockSpec((1,) + b2_row.shape[1:], lambda kv, i, pt: (kv, 0, 0)),
        ],
        out_specs=pl.BlockSpec((1, 1, out_rows, KV_COLS), lambda kv, i, pt: (kv, i, 0, 0)),
        scratch_shapes=[pltpu.VMEM((2, n_pages, KV_COLS, PAGE_SIZE), F32),
                        pltpu.VMEM((rows, sub_w), BF16),
                        pltpu.SemaphoreType.DMA((2,))],
    )
    return pl.pallas_call(
        functools.partial(_compress_paged_body, n_pages=n_pages, out_rows=out_rows),
        grid_spec=grid_spec,
        out_shape=jax.ShapeDtypeStruct((2, n, out_rows, KV_COLS), BF16),
        compiler_params=_params("arbitrary", "arbitrary"),
        name="compress_paged",
    )(page_table, pool_k, pool_v, new_sub, perm, w1cat, b1_row, w2bd, b2_row)


PAGES_PER_TILE = SEL_TK // PAGE_SIZE


def _nsa_sample_paged_body(pt_ref, pool_k, pool_v, q_ref, kcc_ref, vcc_ref, ksn_ref, vsn_ref,
                           kwc_ref, vwc_ref, kwn_ref, vwn_ref, gates_ref, ovt_ref, zb_ref, o_ref,
                           kbuf, vbuf, ksem, vsem, *, n_pages, n_new, win_buf):
    i = pl.program_id(0)
    n_seq = pl.num_programs(0)
    slot = i % 2
    past_len = n_pages * PAGE_SIZE

    def start(seq, slot_idx):
        _start_pages(pt_ref, seq, pool_k, kbuf, slot_idx, ksem, n_pages)
        _start_pages(pt_ref, seq, pool_v, vbuf, slot_idx, vsem, n_pages)

    @pl.when(i == 0)
    def _():
        start(i, slot)

    @pl.when(i + 1 < n_seq)
    def _():
        start(i + 1, 1 - slot)

    rows = N_HEADS * n_new
    grows = GROUP * n_new
    n_cmp = kcc_ref.shape[2]
    n_selp = ovt_ref.shape[0]
    qb = q_ref[0]
    row = lax.broadcasted_iota(jnp.int32, (rows, 1), 0)
    q_pos = past_len + row % n_new

    s = _dot_nt(qb, kcc_ref[0, 0])
    cmp_end = lax.broadcasted_iota(jnp.int32, (1, n_cmp), 1) * CMP_STRIDE + (CMP_BLOCK - 1)
    p_c = _softmax_rows(s, cmp_end <= q_pos)
    o_c = jnp.dot(p_c.astype(BF16), vcc_ref[0, 0], preferred_element_type=F32)

    parts = []
    for g in range(N_KV):
        acc = p_c[g * grows:g * grows + n_new]
        for r in range(1, GROUP):
            acc = acc + p_c[g * grows + r * n_new:g * grows + (r + 1) * n_new]
        parts.extend([acc] * GROUP)
    p_sum = jnp.concatenate(parts, axis=0)
    imp = _dot_nt(ovt_ref[...], p_sum, precision=lax.Precision.HIGHEST)
    jj = lax.broadcasted_iota(jnp.int32, (n_selp, rows), 0)
    cur = (past_len + lax.broadcasted_iota(jnp.int32, (n_selp, rows), 1) % n_new) // SEL_BLOCK
    forced = (jj == 0) | (jj == cur) | (jj == cur - 1)
    imp = jnp.where(forced, FORCE, imp)
    imp = jnp.where(jj <= cur, imp, NEG)
    sel = jnp.zeros((n_selp, rows), F32)
    n_sel = (past_len + n_new + SEL_BLOCK - 1) // SEL_BLOCK
    for _ in range(min(N_SEL, n_sel)):
        mx = jnp.max(imp, axis=0, keepdims=True)
        first = jnp.min(jnp.where(imp == mx, jj, n_selp), axis=0, keepdims=True)
        pick = jj == first
        sel = jnp.where(pick, 1.0, sel)
        imp = jnp.where(pick, BELOW_NEG, imp)
    qa = jnp.concatenate([qb, jnp.where(sel.T > 0.0, 0.0, NEG).astype(BF16)], axis=1)

    def attend(carry, kt, vt, k0, width, causal):
        m, l, acc = carry
        st = jnp.dot(qa, jnp.concatenate([kt, zb_ref[:, pl.ds(k0, width)]], axis=0), preferred_element_type=F32)
        if causal:
            key = k0 + lax.broadcasted_iota(jnp.int32, (1, width), 1)
            st = jnp.where(key <= q_pos, st, NEG)
        m_new = jnp.maximum(m, jnp.max(st, axis=-1, keepdims=True))
        alpha = jnp.exp(m - m_new)
        p = jnp.exp(st - m_new)
        l = alpha * l + jnp.sum(p, axis=-1, keepdims=True)
        acc = alpha * acc + _dot_nt(p.astype(BF16), vt)
        return m_new, l, acc

    _wait_pages(pool_k, kbuf, slot, ksem, n_pages)
    _wait_pages(pool_v, vbuf, slot, vsem, n_pages)

    def sel_tile(c, carry):
        pages = lambda buf: jnp.concatenate(
            [buf[slot, c * PAGES_PER_TILE + p].astype(BF16) for p in range(PAGES_PER_TILE)], axis=1)
        return attend(carry, pages(kbuf), pages(vbuf), pl.multiple_of(c * SEL_TK, SEL_TK), SEL_TK, False)

    carry = (jnp.full((rows, 1), NEG, F32), jnp.zeros((rows, 1), F32), jnp.zeros((rows, KV_COLS), F32))
    carry = lax.fori_loop(0, n_pages // PAGES_PER_TILE, sel_tile, carry, unroll=8)
    _, l_s, acc_s = attend(carry, ksn_ref[0], vsn_ref[0], past_len, ksn_ref.shape[2], True)
    o_s = acc_s / l_s

    kwt = jnp.concatenate([kwc_ref[0].astype(BF16), kwn_ref[0]], axis=1)
    vwt = jnp.concatenate([vwc_ref[0].astype(BF16), vwn_ref[0]], axis=1)
    sw = jnp.dot(qb, kwt, preferred_element_type=F32)
    kpos = past_len - win_buf + lax.broadcasted_iota(jnp.int32, (1, kwt.shape[1]), 1)
    dist = q_pos - kpos
    p_w = _softmax_rows(sw, (dist >= 0) & (dist < WINDOW) & (kpos >= 0))
    o_w = _dot_nt(p_w.astype(BF16), vwt)

    gt = gates_ref[0]
    o = gt[:, 0:1] * o_c + gt[:, 1:2] * o_s + gt[:, 2:3] * o_w
    diag = lax.broadcasted_iota(jnp.int32, (rows, KV_COLS), 1) // HEAD_DIM == row // grows
    o_ref[0] = jnp.where(diag, o, 0.0)


def _nsa_sample_paged(page_table, pool_k, pool_v, qbd, kvcc, ksn, vsn, kwc, vwc, kwn, vwn, gate_rows, *, n_new, win_buf):
    n, n_pages = page_table.shape
    n_cmp = kvcc.shape[2]
    n_sel = (n_pages * PAGE_SIZE + n_new + SEL_BLOCK - 1) // SEL_BLOCK
    n_selp = (n_sel + LANES - 1) // LANES * LANES
    assert n_pages % PAGES_PER_TILE == 0 and n_pages >= PAGES_PER_TILE
    ovt = jnp.asarray(_overlap_matrix(n_cmp, n_selp).T)
    n_keys = n_pages * PAGE_SIZE + ksn.shape[2]
    zb = jnp.asarray(np.arange(n_selp)[:, None] == np.arange(n_keys)[None, :] // SEL_BLOCK, BF16)
    per = lambda a: pl.BlockSpec((1,) + a.shape[1:], lambda i, pt: (i,) + (0,) * (a.ndim - 1))
    grid_spec = pltpu.PrefetchScalarGridSpec(
        num_scalar_prefetch=1,
        grid=(n,),
        in_specs=[
            pl.BlockSpec(memory_space=pl.ANY), pl.BlockSpec(memory_space=pl.ANY),
            per(qbd),
            pl.BlockSpec((1, 1) + kvcc.shape[2:], lambda i, pt: (0, i, 0, 0)),
            pl.BlockSpec((1, 1) + kvcc.shape[2:], lambda i, pt: (1, i, 0, 0)),
            per(ksn), per(vsn), per(kwc), per(vwc), per(kwn), per(vwn), per(gate_rows),
            pl.BlockSpec(ovt.shape, lambda i, pt: (0, 0)),
            pl.BlockSpec(zb.shape, lambda i, pt: (0, 0)),
        ],
        out_specs=pl.BlockSpec((1, N_HEADS * n_new, KV_COLS), lambda i, pt: (i, 0, 0)),
        scratch_shapes=[pltpu.VMEM((2, n_pages, KV_COLS, PAGE_SIZE), F32),
                        pltpu.VMEM((2, n_pages, KV_COLS, PAGE_SIZE), F32),
                        pltpu.SemaphoreType.DMA((2,)), pltpu.SemaphoreType.DMA((2,))],
    )
    return pl.pallas_call(
        functools.partial(_nsa_sample_paged_body, n_pages=n_pages, n_new=n_new, win_buf=win_buf),
        grid_spec=grid_spec,
        out_shape=jax.ShapeDtypeStruct((n, N_HEADS * n_new, KV_COLS), F32),
        compiler_params=_params("arbitrary"),
        name="nsa_sample_paged",
    )(page_table, pool_k, pool_v, qbd, kvcc, kvcc, ksn, vsn, kwc, vwc, kwn, vwn, gate_rows, ovt, zb)


def _merge_body(x_ref, on_ref, gu_ref, v_ref, sga_ref, sgb_ref, wsg_ref, bsg_ref,
                wpa_ref, wpb_ref, wo_ref, g1_ref, b1_ref, wr_ref, br_ref,
                h_ref, route_ref, *, alpha):
    tm = x_ref.shape[0]
    lane = lax.broadcasted_iota(jnp.int32, (CHUNK, LANES), 1)
    tri = lax.broadcasted_iota(jnp.int32, (CHUNK, CHUNK), 0) >= lax.broadcasted_iota(jnp.int32, (CHUNK, CHUNK), 1)
    zero = jnp.zeros((CHUNK, CHUNK), BF16)
    chunks = []
    for c in range(tm // CHUNK):
        rs = slice(c * CHUNK, (c + 1) * CHUNK)
        cols = []
        for p in range(A_GROUPS // 2):
            cs = slice(p * LANES, (p + 1) * LANES)
            v2 = v_ref[rs, cs].astype(BF16)
            lo = jnp.dot(jnp.where(tri, wsg_ref[0, 2 * p], zero), v2, preferred_element_type=F32)
            hi = jnp.dot(jnp.where(tri, wsg_ref[0, 2 * p + 1], zero), v2, preferred_element_type=F32)
            mixed = jnp.where(lane < A_GDIM, lo, hi) + bsg_ref[0, :, cs]
            cols.append(gu_ref[rs, cs] * mixed)
        chunks.append(jnp.concatenate(cols, axis=1))
    a_out = jnp.concatenate(chunks, axis=0).astype(BF16)
    m = (sga_ref[...] * jnp.dot(a_out, wpa_ref[...], preferred_element_type=F32)
         + sgb_ref[...] * jnp.dot(on_ref[...], wpb_ref[...], preferred_element_type=F32))
    pre = alpha * x_ref[...] + jnp.dot(m.astype(BF16), wo_ref[...], preferred_element_type=F32)
    h = _layer_norm_rows(pre, g1_ref[...], b1_ref[...])
    h_ref[...] = h

    lg = jnp.dot(h, wr_ref[...], preferred_element_type=F32, precision=lax.Precision.HIGHEST) + br_ref[...]
    col = lax.broadcasted_iota(jnp.int32, (tm, LANES), 1)
    is_g = col < N_GROUPS
    gl = jnp.where(is_g, lg, BELOW_NEG)
    gmax = jnp.max(gl, axis=1, keepdims=True)
    grp = jnp.min(jnp.where(gl == gmax, col, LANES), axis=1, keepdims=True)
    p_grp = 1.0 / jnp.sum(jnp.where(is_g, jnp.exp(gl - gmax), 0.0), axis=1, keepdims=True)
    e_idx = col - N_GROUPS
    in_grp = (e_idx >= 0) & (e_idx < N_EXPERTS) & (e_idx // EXPERTS_PER_GROUP == grp)
    el = jnp.where(in_grp, lg, BELOW_NEG)
    m1 = jnp.max(el, axis=1, keepdims=True)
    i1 = jnp.min(jnp.where(el == m1, col, LANES), axis=1, keepdims=True)
    el2 = jnp.where(col == i1, BELOW_NEG, el)
    m2 = jnp.max(el2, axis=1, keepdims=True)
    i2 = jnp.min(jnp.where(el2 == m2, col, LANES), axis=1, keepdims=True)
    e2 = jnp.exp(m2 - m1)
    w1 = p_grp / (1.0 + e2)
    w2 = p_grp * e2 / (1.0 + e2)
    route = jnp.where(col == 0, (i1 - N_GROUPS).astype(F32),
                      jnp.where(col == 1, (i2 - N_GROUPS).astype(F32),
                                jnp.where(col == 2, w1, jnp.where(col == 3, w2, 0.0))))
    route_ref[...] = route


def _merge(x, o_nsa, gu, v, sga, sgb, wsg, bsg, wpa, wpb, wo, g1, b1, wr, br, *, alpha, n_first):
    t = x.shape[0]
    tm = MERGE_TM
    assert t % tm == 0
    row = lambda n: pl.BlockSpec((tm, n), lambda i: (i, 0))
    full = lambda a: pl.BlockSpec(a.shape, lambda i: (0,) * a.ndim)
    pick = lambda a: pl.BlockSpec((1,) + a.shape[1:], lambda i: (jnp.where(i < n_first, 0, 1),) + (0,) * (a.ndim - 1))
    return pl.pallas_call(
        functools.partial(_merge_body, alpha=alpha),
        grid=(t // tm,),
        in_specs=[row(D_MODEL), row(D_MODEL), row(A_HALF), row(A_HALF), row(D_MODEL), row(D_MODEL),
                  pick(wsg), pick(bsg), full(wpa), full(wpb), full(wo), full(g1), full(b1), full(wr), full(br)],
        out_specs=[row(D_MODEL), row(LANES)],
        out_shape=[jax.ShapeDtypeStruct((t, D_MODEL), F32), jax.ShapeDtypeStruct((t, LANES), F32)],
        compiler_params=_params("parallel"),
        name="merge",
    )(x, o_nsa, gu, v, sga, sgb, wsg, bsg, wpa, wpb, wo, g1, b1, wr, br)


def _row_copy(h_ref, tok, xbuf, slot, r, sem):
    return pltpu.make_async_copy(h_ref.at[pl.ds(tok, 1)], xbuf.at[slot, pl.ds(r, 1)], sem.at[slot])


def _experts_body(be_ref, st_ref, nb_ref, h_ref, wg_ref, wu_ref, wd_ref, o_ref, xbuf, sem):
    i = pl.program_id(0)
    bm = xbuf.shape[1]
    slot = i % 2
    n_used = nb_ref[0]

    def start(blk, slot_idx):
        def body(r, c):
            _row_copy(h_ref, st_ref[blk * bm + r], xbuf, slot_idx, r, sem).start()
            return c
        lax.fori_loop(0, bm, body, 0, unroll=8)

    @pl.when((i == 0) & (n_used > 0))
    def _():
        start(i, slot)

    @pl.when(i + 1 < n_used)
    def _():
        start(i + 1, 1 - slot)

    @pl.when(i < n_used)
    def _():
        def body(r, c):
            _row_copy(h_ref, 0, xbuf, slot, r, sem).wait()
            return c
        lax.fori_loop(0, bm, body, 0, unroll=8)
        x = xbuf[slot].astype(BF16)
        gate = jnp.dot(x, wg_ref[0], preferred_element_type=F32)
        up = jnp.dot(x, wu_ref[0], preferred_element_type=F32)
        hid = (jax.nn.silu(gate) * up).astype(BF16)
        o_ref[...] = jnp.dot(hid, wd_ref[0], preferred_element_type=F32)

    @pl.when(i >= n_used)
    def _():
        o_ref[...] = jnp.zeros(o_ref.shape, F32)


def _experts(blk_exp, slot_tok, n_used, h, wg, wu, wd):
    n_slots = slot_tok.shape[0]
    bm = MOE_BM
    grid_spec = pltpu.PrefetchScalarGridSpec(
        num_scalar_prefetch=3,
        grid=(n_slots // bm,),
        in_specs=[
            pl.BlockSpec(memory_space=pl.ANY),
            pl.BlockSpec((1, D_MODEL, D_EXPERT), lambda i, be, st, nb: (be[i], 0, 0)),
            pl.BlockSpec((1, D_MODEL, D_EXPERT), lambda i, be, st, nb: (be[i], 0, 0)),
            pl.BlockSpec((1, D_EXPERT, D_MODEL), lambda i, be, st, nb: (be[i], 0, 0)),
        ],
        out_specs=pl.BlockSpec((bm, D_MODEL), lambda i, be, st, nb: (i, 0)),
        scratch_shapes=[pltpu.VMEM((2, bm, D_MODEL), F32), pltpu.SemaphoreType.DMA((2,))],
    )
    return pl.pallas_call(
        _experts_body,
        grid_spec=grid_spec,
        out_shape=jax.ShapeDtypeStruct((n_slots, D_MODEL), F32),
        compiler_params=_params("arbitrary"),
        name="experts",
    )(blk_exp, slot_tok, n_used, h, wg, wu, wd)


def _final_body(h_ref, y_ref, g_ref, b_ref, o_ref, *, alpha):
    o_ref[...] = _layer_norm_rows(alpha * h_ref[...] + y_ref[...], g_ref[...], b_ref[...])


def _final_norm(h, y, g, b, *, alpha):
    t = h.shape[0]
    tm = MERGE_TM
    row = pl.BlockSpec((tm, D_MODEL), lambda i: (i, 0))
    full = lambda a: pl.BlockSpec(a.shape, lambda i: (0,) * a.ndim)
    return pl.pallas_call(
        functools.partial(_final_body, alpha=alpha),
        grid=(t // tm,),
        in_specs=[row, row, full(g), full(b)],
        out_specs=row,
        out_shape=jax.ShapeDtypeStruct((t, D_MODEL), F32),
        compiler_params=_params("parallel"),
        name="final_norm",
    )(h, y, g, b)


def _moe_dispatch(route):
    t = route.shape[0]
    n_assign = t * TOP_K
    flat_e = route[:, 0:TOP_K].astype(jnp.int32).reshape(-1)
    onehot = (flat_e[:, None] == jnp.arange(N_EXPERTS, dtype=jnp.int32)[None, :]).astype(jnp.int32)
    csum = jnp.cumsum(onehot, axis=0)
    counts = csum[-1]
    rank = jnp.sum(onehot * csum, axis=1) - 1
    padded = (counts + MOE_BM - 1) // MOE_BM * MOE_BM
    pad_end = jnp.cumsum(padded)
    pad_start = pad_end - padded
    start = jnp.cumsum(counts) - counts
    dest = (jnp.sum(onehot * pad_start[None, :], axis=1) + rank).reshape(t, TOP_K)
    n_blocks = -(-n_assign // MOE_BM) + N_EXPERTS
    blk_start = jnp.arange(n_blocks, dtype=jnp.int32) * MOE_BM
    blk_exp = jnp.minimum(jnp.sum(pad_end[None, :] <= blk_start[:, None], axis=1), N_EXPERTS - 1).astype(jnp.int32)
    order = jnp.argsort(flat_e)
    e_slot = jnp.repeat(blk_exp, MOE_BM)
    off = jnp.arange(n_blocks * MOE_BM, dtype=jnp.int32) - pad_start[e_slot]
    src = jnp.clip(start[e_slot] + off, 0, n_assign - 1)
    slot_tok = jnp.where(off < counts[e_slot], order[src].astype(jnp.int32) // TOP_K, 0)
    n_used = (pad_end[-1:] // MOE_BM).astype(jnp.int32)
    return slot_tok, dest, blk_exp, n_used


def _nsa_sample_glue(page_table, cmp_k, cmp_v, sel_k, sel_v, win_k, win_v, q, kc, vc, ksb, vsb, kwb, vwb, gates, cw, *, n_new):
    n, n_pages = page_table.shape
    ts = n * n_new
    win_buf = win_k.shape[1]
    new = lambda a: a.reshape(n, n_new, KV_COLS)
    cmp_out = (n_pages * (PAGE_SIZE // CMP_STRIDE) + 16 + LANES - 1) // LANES * LANES
    kvcc = _compress_paged(page_table, _page_view(cmp_k), _page_view(cmp_v), jnp.stack([new(kc), new(vc)]), cw, out_rows=cmp_out)
    new_t = lambda a: jnp.pad(new(a).transpose(0, 2, 1), ((0, 0), (0, 0), (0, LANES - n_new)))
    q_s = q.reshape(n, n_new, N_KV, GROUP, HEAD_DIM)
    qbd = jnp.einsum('ntgrd,gj->ngrtjd', q_s, jnp.eye(N_KV, dtype=BF16)).reshape(n, N_HEADS * n_new, KV_COLS)
    g_s = gates.reshape(n, n_new, N_KV, LANES)[..., :3 * GROUP].reshape(n, n_new, N_KV, 3, GROUP)
    gate_rows = jnp.pad(g_s.transpose(0, 2, 4, 1, 3).reshape(n, N_HEADS * n_new, 3), ((0, 0), (0, 0), (0, LANES - 3)))
    od = _nsa_sample_paged(page_table, _page_view(sel_k), _page_view(sel_v), qbd, kvcc,
                           new_t(ksb), new_t(vsb), _page_view(win_k), _page_view(win_v),
                           new_t(kwb), new_t(vwb), gate_rows, n_new=n_new, win_buf=win_buf)
    o_s = jnp.einsum('ngrtgd->ntgrd', od.reshape(n, N_KV, GROUP, n_new, N_KV, HEAD_DIM))
    return o_s.reshape(ts, N_HEADS * HEAD_DIM).astype(BF16)


def _sgu_weights(sgu_w, sgu_b, n_new):
    reps = CHUNK // n_new
    w_short = jnp.einsum('ab,gts->gatbs', jnp.eye(reps, dtype=F32), sgu_w[:, :n_new, :n_new]).reshape(A_GROUPS, CHUNK, CHUNK)
    wsg = jnp.stack([sgu_w[:, :CHUNK, :CHUNK], w_short]).astype(BF16)
    b_full = jnp.repeat(sgu_b[:, :CHUNK].T, A_GDIM, axis=1)
    b_short = jnp.tile(jnp.repeat(sgu_b[:, :n_new].T, A_GDIM, axis=1), (reps, 1))
    return wsg, jnp.stack([b_full, b_short])


def kernel(x_prompt, x_sample, cache_cmp_k, cache_cmp_v, cache_sel_k, cache_sel_v, cache_win_k, cache_win_v, page_table, w_in, cmp_pe, cmp_w1, cmp_b1, cmp_w2, cmp_b2, sgu_ln_g, sgu_ln_b, sgu_w, sgu_b, w_proj_a, w_proj_b, w_out, ln1_g, ln1_b, router_group_w, router_group_b, router_expert_w, router_expert_b, exp_w_gate, exp_w_up, exp_w_down, ln2_g, ln2_b):
    depth = w_in.shape[0]
    alpha = (2.0 * depth) ** 0.25
    b, t, _ = x_prompt.shape
    n, n_new, _ = x_sample.shape
    tp, ts = b * t, n * n_new
    past_len = page_table.shape[1] * PAGE_SIZE
    win_buf = cache_win_k.shape[2]
    assert CHUNK % n_new == 0 and tp % MERGE_TM == 0 and ts % MERGE_TM == 0

    x_p, x_s = x_prompt, x_sample
    states = []
    for l in range(depth):
        x_all = jnp.concatenate([x_p.reshape(tp, D_MODEL), x_s.reshape(ts, D_MODEL)], axis=0)
        (q, kc, vc, ks, vs, kw, vw, ksb, vsb, kwb, vwb, gu, v, sga, sgb, gates) = _in_proj(
            x_all, _pack_w_in(w_in[l]), sgu_ln_g[l][None], sgu_ln_b[l][None])
        cw = _compress_weights(cmp_pe[l], cmp_w1[l], cmp_b1[l], cmp_w2[l], cmp_b2[l])

        sub_w = CMP_STRIDE * KV_COLS
        kcc, vcc = _compress(kc[:tp].reshape(b, t // CMP_STRIDE, sub_w), vc[:tp].reshape(b, t // CMP_STRIDE, sub_w),
                             cw, tile=128, out_rows=t // CMP_STRIDE)
        seq = lambda a: a[:tp].reshape(b, t, -1)
        o_p = _nsa_prompt_glue(seq(q), kcc, vcc, seq(ksb), seq(vsb), seq(kwb), seq(vwb), seq(gates))

        tail = lambda a: a[tp:]
        o_s = _nsa_sample_glue(page_table, cache_cmp_k[l], cache_cmp_v[l], cache_sel_k[l], cache_sel_v[l],
                               cache_win_k[l], cache_win_v[l], tail(q), tail(kc), tail(vc), tail(ksb), tail(vsb),
                               tail(kwb), tail(vwb), tail(gates), cw, n_new=n_new)
        o_all = jnp.concatenate([o_p.reshape(tp, -1), o_s], axis=0)

        wsg, bsg = _sgu_weights(sgu_w[l], sgu_b[l], n_new)
        wr = jnp.pad(jnp.concatenate([router_group_w[l], router_expert_w[l]], axis=1), ((0, 0), (0, LANES - N_GROUPS - N_EXPERTS)))
        br = jnp.pad(jnp.concatenate([router_group_b[l], router_expert_b[l]]), (0, LANES - N_GROUPS - N_EXPERTS))[None]
        h, route = _merge(x_all, o_all, gu, v, sga, sgb, wsg, bsg,
                              w_proj_a[l].astype(BF16), w_proj_b[l].astype(BF16), w_out[l].astype(BF16),
                              ln1_g[l][None], ln1_b[l][None], wr, br, alpha=alpha, n_first=tp // MERGE_TM)

        slot_tok, dest, blk_exp, n_used = _moe_dispatch(route)
        out = _experts(blk_exp, slot_tok, n_used, h,
                       exp_w_gate[l].astype(BF16), exp_w_up[l].astype(BF16), exp_w_down[l].astype(BF16))
        y_moe = route[:, 2:3] * out[dest[:, 0]] + route[:, 3:4] * out[dest[:, 1]]
        y_all = _final_norm(h, y_moe, ln2_g[l][None], ln2_b[l][None], alpha=alpha)

        kv5 = lambda a, lo, hi, bb: a[lo:hi].reshape(bb, -1, N_KV, HEAD_DIM)
        keep_p = min(WINDOW, t)
        prompt_state = tuple(kv5(a, 0, tp, b) for a in (kc, vc, ks, vs)) + tuple(
            kv5(a, 0, tp, b)[:, t - keep_p:] for a in (kw, vw))
        keep_s = min(WINDOW, win_buf + n_new)
        win_all = lambda cache, a: jnp.concatenate([cache[l], kv5(a, tp, tp + ts, n)], axis=1)[:, win_buf + n_new - keep_s:]
        sample_state = tuple(kv5(a, tp, tp + ts, n) for a in (kc, vc, ks, vs)) + (
            win_all(cache_win_k, kw), win_all(cache_win_v, vw), v[tp:].reshape(n, n_new, A_HALF))
        states.append(prompt_state + sample_state)
        x_p, x_s = y_all[:tp].reshape(b, t, D_MODEL), y_all[tp:].reshape(n, n_new, D_MODEL)
    return (x_p, x_s) + tuple(jnp.stack(z) for z in zip(*states))
```

```python
import functools

import numpy as np
import jax
import jax.numpy as jnp
from jax import lax
from jax.experimental import pallas as pl
from jax.experimental.pallas import tpu as pltpu

F32 = jnp.float32
BF16 = jnp.bfloat16

D_MODEL = 1024
N_HEADS = 16
HEAD_DIM = 64
N_KV = 4
GROUP = N_HEADS // N_KV
KV_COLS = N_KV * HEAD_DIM
CMP_STRIDE = 16
CMP_BLOCK = 32
CMP_HIDDEN = 64
SEL_BLOCK = 64
N_SEL = 16
WINDOW = 512
Q_BLOCK = 128
PAGE_SIZE = 128
CHUNK = 128
A_GROUPS = 8
A_HALF = D_MODEL // 2
A_GDIM = A_HALF // A_GROUPS
N_GROUPS = 4
EXPERTS_PER_GROUP = 8
N_EXPERTS = N_GROUPS * EXPERTS_PER_GROUP
TOP_K = 2
D_EXPERT = D_MODEL // 2
LN_EPS = 1e-5
NEG = -1e30
FORCE = 1e9
BELOW_NEG = -3e38

LANES = 128
VMEM_LIMIT = 56 * 1024 * 1024

INPROJ_TM = 256
MERGE_TM = 512
MOE_BM = 256
SEL_TK = 512
SEL_UNROLL = 4


def _gelu(x):
    return jax.nn.gelu(x, approximate=True)


def _layer_norm_rows(x, g, b):
    mu = jnp.mean(x, axis=-1, keepdims=True)
    xc = x - mu
    var = jnp.mean(xc * xc, axis=-1, keepdims=True)
    return xc * lax.rsqrt(var + LN_EPS) * g + b


def _params(*sem):
    return pltpu.CompilerParams(dimension_semantics=tuple(sem), vmem_limit_bytes=VMEM_LIMIT)


_C_Q = 0
_C_KV = D_MODEL
_C_AU = _C_KV + 6 * KV_COLS
_C_AV = _C_AU + A_HALF
_C_GA = _C_AV + A_HALF
_C_GB = _C_GA + D_MODEL
_C_GATE = _C_GB + D_MODEL
_C_END = _C_GATE + N_KV * LANES


def _pack_w_in(w_in):
    q_cols = N_HEADS * HEAD_DIM
    o_gate = q_cols + 6 * KV_COLS
    o_au = o_gate + 3 * N_HEADS
    src = np.zeros((N_KV * LANES,), np.int32)
    valid = np.zeros((N_KV * LANES,), np.float32)
    for g in range(N_KV):
        for c in range(3):
            for r in range(GROUP):
                src[g * LANES + c * GROUP + r] = o_gate + (g * GROUP + r) * 3 + c
                valid[g * LANES + c * GROUP + r] = 1.0
    w_gate = w_in[:, src] * jnp.asarray(valid)
    packed = jnp.concatenate([w_in[:, :o_gate], w_in[:, o_au:], w_gate], axis=1)
    return packed.astype(BF16)


def _inproj_body(x_ref, w_ref, lng_ref, lnb_ref,
                 q_ref, kc_ref, vc_ref, ks_ref, vs_ref, kw_ref, vw_ref,
                 ksb_ref, vsb_ref, kwb_ref, vwb_ref,
                 gu_ref, v_ref, sga_ref, sgb_ref, gates_ref):
    x = x_ref[...].astype(BF16)

    def mm(c0, n):
        return jnp.dot(x, w_ref[:, c0:c0 + n], preferred_element_type=F32)

    q_ref[...] = (mm(_C_Q, D_MODEL) * (HEAD_DIM ** -0.5)).astype(BF16)
    f32_refs = (kc_ref, vc_ref, ks_ref, vs_ref, kw_ref, vw_ref)
    bf_refs = (None, None, ksb_ref, vsb_ref, kwb_ref, vwb_ref)
    for i in range(6):
        y = mm(_C_KV + i * KV_COLS, KV_COLS)
        f32_refs[i][...] = y
        if bf_refs[i] is not None:
            bf_refs[i][...] = y.astype(BF16)
    gu_ref[...] = _gelu(mm(_C_AU, A_HALF))
    v_ref[...] = _layer_norm_rows(_gelu(mm(_C_AV, A_HALF)), lng_ref[...], lnb_ref[...])
    sga_ref[...] = jax.nn.sigmoid(mm(_C_GA, D_MODEL))
    sgb_ref[...] = jax.nn.sigmoid(mm(_C_GB, D_MODEL))
    gates_ref[...] = jax.nn.sigmoid(mm(_C_GATE, N_KV * LANES))


def _in_proj(x, w_packed, ln_g, ln_b):
    t = x.shape[0]
    tm = INPROJ_TM
    assert t % tm == 0
    row = lambda n: pl.BlockSpec((tm, n), lambda i: (i, 0))
    full = lambda a: pl.BlockSpec(a.shape, lambda i: (0,) * a.ndim)
    out_shapes = (
        [jax.ShapeDtypeStruct((t, D_MODEL), BF16)]
        + [jax.ShapeDtypeStruct((t, KV_COLS), F32)] * 6
        + [jax.ShapeDtypeStruct((t, KV_COLS), BF16)] * 4
        + [jax.ShapeDtypeStruct((t, A_HALF), F32)] * 2
        + [jax.ShapeDtypeStruct((t, D_MODEL), F32)] * 2
        + [jax.ShapeDtypeStruct((t, N_KV * LANES), F32)]
    )
    out_specs = [row(s.shape[1]) for s in out_shapes]
    return pl.pallas_call(
        _inproj_body,
        grid=(t // tm,),
        in_specs=[row(D_MODEL), full(w_packed), full(ln_g), full(ln_b)],
        out_specs=out_specs,
        out_shape=out_shapes,
        compiler_params=_params("parallel"),
        name="in_proj",
    )(x, w_packed, ln_g, ln_b)


def _compress_weights(pe, w1, b1, w2, b2):
    eye = jnp.eye(N_KV, dtype=F32)
    w1r = w1.reshape(2, 2, CMP_STRIDE, HEAD_DIM, CMP_HIDDEN)
    w1bd = jnp.einsum('kfsdh,gj->kfsgdjh', w1r, eye).reshape(
        2, 2, CMP_STRIDE * KV_COLS, N_KV * CMP_HIDDEN).astype(BF16)
    per = pe.reshape(2, 2, CMP_STRIDE, 1, HEAD_DIM)
    pe_row = jnp.broadcast_to(per, (2, 2, CMP_STRIDE, N_KV, HEAD_DIM)).reshape(2, 2, 1, CMP_STRIDE * KV_COLS)
    b1_row = jnp.tile(b1, (1, N_KV)).reshape(2, 1, N_KV * CMP_HIDDEN)
    w2bd = jnp.einsum('khd,gj->kghjd', w2, eye).reshape(2, N_KV * CMP_HIDDEN, KV_COLS).astype(BF16)
    b2_row = jnp.tile(b2, (1, N_KV)).reshape(2, 1, KV_COLS)
    w1pair = jnp.einsum('kfsdh,ab->ksadfbh', w1r, jnp.eye(2, dtype=F32)).reshape(
        2, CMP_STRIDE * KV_COLS // 2, 2 * 2 * CMP_HIDDEN).astype(BF16)
    return pe_row, w1bd, b1_row, w2bd, b2_row, w1pair


def _compress_body(xk_ref, xv_ref, pe_ref, w1_ref, b1_ref, w2_ref, b2_ref, ok_ref, ov_ref,
                   first_ref, second_ref, *, rows, tile, out_rows):
    j = pl.program_id(1)
    r0 = pl.multiple_of(j * tile, 8)
    for kv, x_ref in enumerate((xk_ref, xv_ref)):
        x = x_ref[0]
        first_ref[kv, pl.ds(r0, tile), :] = jnp.dot(
            (x + pe_ref[kv, 0]).astype(BF16), w1_ref[kv, 0], preferred_element_type=F32)
        second_ref[kv, pl.ds(r0, tile), :] = jnp.dot(
            (x + pe_ref[kv, 1]).astype(BF16), w1_ref[kv, 1], preferred_element_type=F32)

    @pl.when(j == pl.num_programs(1) - 1)
    def _():
        for kv, o_ref in enumerate((ok_ref, ov_ref)):
            nxt = pltpu.roll(second_ref[kv], rows - 1, axis=0)
            hid = _gelu(first_ref[kv] + nxt + b1_ref[kv])
            out = jnp.dot(hid.astype(BF16), w2_ref[kv], preferred_element_type=F32) + b2_ref[kv]
            o_ref[0, 0:rows, :] = out.astype(BF16)
            if out_rows > rows:
                o_ref[0, rows:out_rows, :] = jnp.zeros((out_rows - rows, KV_COLS), BF16)


def _compress(xk, xv, cw, *, tile, out_rows):
    pe_row, w1bd, b1_row, w2bd, b2_row, _ = cw
    n, rows, width = xk.shape
    assert rows % tile == 0 and tile % 8 == 0
    xspec = pl.BlockSpec((1, tile, width), lambda i, j: (i, j, 0))
    full = lambda a: pl.BlockSpec(a.shape, lambda i, j: (0,) * a.ndim)
    ospec = pl.BlockSpec((1, out_rows, KV_COLS), lambda i, j: (i, 0, 0))
    oshape = jax.ShapeDtypeStruct((n, out_rows, KV_COLS), BF16)
    return pl.pallas_call(
        functools.partial(_compress_body, rows=rows, tile=tile, out_rows=out_rows),
        grid=(n, rows // tile),
        in_specs=[xspec, xspec, full(pe_row), full(w1bd), full(b1_row), full(w2bd), full(b2_row)],
        out_specs=[ospec, ospec],
        out_shape=[oshape, oshape],
        scratch_shapes=[pltpu.VMEM((2, rows, N_KV * CMP_HIDDEN), F32),
                        pltpu.VMEM((2, rows, N_KV * CMP_HIDDEN), F32)],
        compiler_params=_params("parallel", "arbitrary"),
        name="compress",
    )(xk, xv, pe_row, w1bd, b1_row, w2bd, b2_row)


def _softmax_rows(s, mask):
    s = jnp.where(mask, s, NEG)
    e = jnp.exp(s - jnp.max(s, axis=-1, keepdims=True)) * mask.astype(F32)
    return e / jnp.maximum(jnp.sum(e, axis=-1, keepdims=True), 1e-30)


def _dot_nt(a, b, **kw):
    return lax.dot_general(a, b, (((1,), (1,)), ((), ())), preferred_element_type=F32, **kw)


def _overlap_matrix(n_cmp_rows, n_sel_cols):
    c_start = np.arange(n_cmp_rows)[:, None] * CMP_STRIDE
    s_start = np.arange(n_sel_cols)[None, :] * SEL_BLOCK
    return ((c_start < s_start + SEL_BLOCK) & (c_start + CMP_BLOCK > s_start)).astype(np.float32)


def _nsa_prompt_body(q_ref, kcc_ref, vcc_ref, ks_ref, vs_ref, kw_ref, vw_ref, gates_ref, ovt_ref, wb_ref, o_ref):
    i = pl.program_id(2)
    tq = Q_BLOCK
    rows = GROUP * tq
    n_cmp = kcc_ref.shape[2]
    n_sel = ovt_ref.shape[0]
    q2 = q_ref[0]
    qs = jnp.concatenate([q2[:, r * HEAD_DIM:(r + 1) * HEAD_DIM] for r in range(GROUP)], axis=0)
    tok = lax.broadcasted_iota(jnp.int32, (rows, 1), 0) % tq
    q_pos = i * tq + tok

    cmp_end = lax.broadcasted_iota(jnp.int32, (1, n_cmp), 1) * CMP_STRIDE + (CMP_BLOCK - 1)
    s = jnp.where(cmp_end <= q_pos, _dot_nt(qs, kcc_ref[0, 0]), NEG)
    e_c = jnp.exp(s - jnp.max(s, axis=-1, keepdims=True))
    p_c = e_c * jnp.where(q_pos >= CMP_BLOCK - 1, 1.0 / jnp.sum(e_c, axis=-1, keepdims=True), 0.0)
    o_c = jnp.dot(p_c.astype(BF16), vcc_ref[0, 0], preferred_element_type=F32)

    wk = WINDOW + tq
    w0 = pl.multiple_of(i * tq, tq)
    kpos = i * tq - WINDOW + lax.broadcasted_iota(jnp.int32, (1, wk), 1)
    sw = _dot_nt(qs, kw_ref[0, 0, pl.ds(w0, wk), :]) + wb_ref[...] + jnp.where(kpos >= 0, 0.0, NEG)
    e_w = jnp.exp(sw - jnp.max(sw, axis=-1, keepdims=True))
    p_w = e_w * (1.0 / jnp.sum(e_w, axis=-1, keepdims=True))
    o_w = jnp.dot(p_w.astype(BF16), vw_ref[0, 0, pl.ds(w0, wk), :], preferred_element_type=F32)

    gt = gates_ref[0]
    gcol = lambda c: jnp.concatenate([gt[:, c * GROUP + r:c * GROUP + r + 1] for r in range(GROUP)], axis=0)
    o_cw = gcol(0) * o_c + gcol(2) * o_w

    p_sum = p_c[0:tq]
    for r in range(1, GROUP):
        p_sum = p_sum + p_c[r * tq:(r + 1) * tq]
    imp = _dot_nt(ovt_ref[...], p_sum, precision=lax.Precision.HIGHEST)
    jj = lax.broadcasted_iota(jnp.int32, (n_sel, tq), 0)
    cur = (i * tq + lax.broadcasted_iota(jnp.int32, (n_sel, tq), 1)) // SEL_BLOCK
    forced = (jj == 0) | (jj == cur) | (jj == cur - 1)
    imp = jnp.where(forced, FORCE, imp)
    imp = jnp.where(jj <= cur, imp, NEG)
    sel = jnp.zeros((n_sel, tq), F32)
    for _ in range(min(N_SEL, n_sel)):
        mx = jnp.max(imp, axis=0, keepdims=True)
        first = jnp.min(jnp.where(imp == mx, jj, n_sel), axis=0, keepdims=True)
        pick = jj == first
        sel = jnp.where(pick, 1.0, sel)
        imp = jnp.where(pick, BELOW_NEG, imp)
    nb = jnp.where(sel.T > 0.0, 0.0, NEG).astype(BF16)
    qa = jnp.concatenate([jnp.concatenate([nb] * GROUP, axis=0), qs], axis=1)

    def sel_tile(c, carry, causal):
        m, l, acc = carry
        k0 = pl.multiple_of(c * SEL_TK, SEL_TK)
        st = _dot_nt(qa, ks_ref[0, 0, pl.ds(k0, SEL_TK), :])
        if causal:
            key = k0 + lax.broadcasted_iota(jnp.int32, (1, SEL_TK), 1)
            st = jnp.where(key <= q_pos, st, NEG)
        m_new = jnp.maximum(m, jnp.max(st, axis=-1, keepdims=True))
        alpha = jnp.exp(m - m_new)
        p = jnp.exp(st - m_new)
        l = alpha * l + jnp.sum(p, axis=-1, keepdims=True)
        acc = alpha * acc + jnp.dot(p.astype(BF16), vs_ref[0, 0, pl.ds(k0, SEL_TK), :], preferred_element_type=F32)
        return m_new, l, acc

    def tile_group(gi, carry):
        for u in range(SEL_UNROLL):
            carry = sel_tile(SEL_UNROLL * gi + u, carry, False)
        return carry

    carry = (jnp.full((rows, 1), NEG, F32), jnp.zeros((rows, 1), F32), jnp.zeros((rows, HEAD_DIM), F32))
    n_full = (i * tq) // SEL_TK
    n_groups = n_full // SEL_UNROLL
    carry = sel_tile(n_full, carry, True)
    carry = lax.fori_loop(0, n_groups, tile_group, carry)
    _, l_s, acc_s = lax.fori_loop(n_groups * SEL_UNROLL, n_full, functools.partial(sel_tile, causal=False), carry)
    o_s = acc_s / l_s

    o = o_cw + gcol(1) * o_s
    for r in range(GROUP):
        o_ref[0, :, r * HEAD_DIM:(r + 1) * HEAD_DIM] = o[r * tq:(r + 1) * tq].astype(BF16)


def _nsa_prompt(q, kcc, vcc, ks, vs, kw, vw, gates):
    b, t, _ = q.shape
    n_cmp = kcc.shape[2]
    n_sel = t // SEL_BLOCK
    assert t % SEL_TK == 0 and t % Q_BLOCK == 0 and ks.shape[3] == n_sel + HEAD_DIM
    ovt = jnp.asarray(_overlap_matrix(n_cmp, n_sel).T)
    tt = np.arange(GROUP * Q_BLOCK)[:, None] % Q_BLOCK
    col = np.arange(WINDOW + Q_BLOCK)[None, :]
    wb = jnp.asarray(np.where((col > tt) & (col <= tt + WINDOW), 0.0, NEG).astype(np.float32))
    per_bg = lambda a: pl.BlockSpec((1, 1) + a.shape[2:], lambda bi, g, i: (bi, g, 0, 0))
    return pl.pallas_call(
        _nsa_prompt_body,
        grid=(b, N_KV, t // Q_BLOCK),
        in_specs=[
            pl.BlockSpec((1, Q_BLOCK, GROUP * HEAD_DIM), lambda bi, g, i: (bi, i, g)),
            per_bg(kcc), per_bg(vcc), per_bg(ks), per_bg(vs), per_bg(kw), per_bg(vw),
            pl.BlockSpec((1, Q_BLOCK, LANES), lambda bi, g, i: (bi, i, g)),
            pl.BlockSpec(ovt.shape, lambda bi, g, i: (0, 0)),
            pl.BlockSpec(wb.shape, lambda bi, g, i: (0, 0)),
        ],
        out_specs=pl.BlockSpec((1, Q_BLOCK, GROUP * HEAD_DIM), lambda bi, g, i: (bi, i, g)),
        out_shape=jax.ShapeDtypeStruct((b, t, N_HEADS * HEAD_DIM), BF16),
        compiler_params=_params("parallel", "parallel", "arbitrary"),
        name="nsa_prompt",
    )(q, kcc, vcc, ks, vs, kw, vw, gates, ovt, wb)


def _to_heads(a):
    b, l, _ = a.shape
    return a.reshape(b, l, N_KV, HEAD_DIM).transpose(0, 2, 1, 3)


def _nsa_prompt_glue(q, kcc, vcc, ks, vs, kw, vw, gates):
    pad = lambda a: jnp.pad(a, ((0, 0), (WINDOW, 0), (0, 0)))
    b, t, _ = ks.shape
    n_sel = t // SEL_BLOCK
    onehot = (jnp.arange(t)[:, None] // SEL_BLOCK == jnp.arange(n_sel)[None, :]).astype(BF16)
    ks_aug = jnp.concatenate([jnp.broadcast_to(onehot, (b, N_KV, t, n_sel)), _to_heads(ks)], axis=3)
    return _nsa_prompt(q, _to_heads(kcc), _to_heads(vcc), ks_aug, _to_heads(vs),
                       _to_heads(pad(kw)), _to_heads(pad(vw)), gates)


def _page_view(pool):
    return jnp.transpose(pool, (0, 2, 3, 1)).reshape(pool.shape[0], KV_COLS, pool.shape[1])


def _page_copy(pool_ref, page, buf_ref, slot, j, sem_ref):
    return pltpu.make_async_copy(pool_ref.at[page], buf_ref.at[slot, j], sem_ref.at[slot])


def _start_pages(pt_ref, seq, pool_ref, buf_ref, slot, sem_ref, n_pages):
    def body(j, c):
        _page_copy(pool_ref, pt_ref[seq, j], buf_ref, slot, j, sem_ref).start()
        return c
    lax.fori_loop(0, n_pages, body, 0, unroll=8)


def _wait_pages(pool_ref, buf_ref, slot, sem_ref, n_pages):
    for j in range(n_pages):
        _page_copy(pool_ref, 0, buf_ref, slot, j, sem_ref).wait()


def _compress_paged_body(pt_ref, pool_k, pool_v, new_ref, perm_ref, w1_ref, b1_ref, w2_ref, b2_ref,
                         o_ref, pbuf, xsub, sem, *, n_pages, out_rows):
    kv = pl.program_id(0)
    i = pl.program_id(1)
    n_seq = pl.num_programs(1)
    step = kv * n_seq + i
    slot = step % 2

    def start(kv_idx, seq, slot_idx):
        @pl.when(kv_idx == 0)
        def _():
            _start_pages(pt_ref, seq, pool_k, pbuf, slot_idx, sem, n_pages)

        @pl.when(kv_idx == 1)
        def _():
            _start_pages(pt_ref, seq, pool_v, pbuf, slot_idx, sem, n_pages)

    @pl.when(step == 0)
    def _():
        start(kv, i, slot)

    @pl.when(i + 1 < n_seq)
    def _():
        start(kv, i + 1, 1 - slot)

    @pl.when((i + 1 == n_seq) & (kv == 0))
    def _():
        start(kv + 1, 0, 1 - slot)

    _wait_pages(pool_k, pbuf, slot, sem, n_pages)

    def pair(p, c):
        xt = jnp.concatenate([pbuf[slot, 2 * p].astype(BF16), pbuf[slot, 2 * p + 1].astype(BF16)], axis=1)
        y = _dot_nt(perm_ref[...], xt)
        r0 = pl.multiple_of(p * 16, 16)
        for s in range(CMP_STRIDE):
            xsub[pl.ds(r0, 16), s * KV_COLS:(s + 1) * KV_COLS] = y[s * 16:(s + 1) * 16].astype(BF16)
        return c
    lax.fori_loop(0, n_pages // 2, pair, 0, unroll=8)
    past_sub = n_pages * (PAGE_SIZE // CMP_STRIDE)
    rows = past_sub + 16
    xsub[past_sub:rows, :] = new_ref[0, 0]

    xs = xsub[...]
    half = KV_COLS // 2
    yp = []
    for p in range(2):
        xp = jnp.concatenate(
            [xs[:, s * KV_COLS + p * half:s * KV_COLS + (p + 1) * half] for s in range(CMP_STRIDE)], axis=1)
        yp.append(jnp.dot(xp, w1_ref[0], preferred_element_type=F32))
    y = jnp.concatenate([yp[0][:, :half], yp[1][:, :half], yp[0][:, half:], yp[1][:, half:]], axis=1)
    h4 = N_KV * CMP_HIDDEN
    pe_first = y[past_sub + 4:past_sub + 5, :h4] + y[past_sub + 5:past_sub + 6, :h4]
    pe_second = y[past_sub + 6:past_sub + 7, h4:] + y[past_sub + 7:past_sub + 8, h4:]
    first = y[:, :h4] + pe_first
    nxt = pltpu.roll(y[:, h4:] + pe_second, rows - 1, axis=0)
    hid = _gelu(first + nxt + b1_ref[0])
    out = jnp.dot(hid.astype(BF16), w2_ref[0], preferred_element_type=F32) + b2_ref[0]
    o_ref[0, 0, 0:rows, :] = out.astype(BF16)
    o_ref[0, 0, rows:out_rows, :] = jnp.zeros((out_rows - rows, KV_COLS), BF16)


def _compress_paged(page_table, pool_k, pool_v, new_kv, cw, *, out_rows):
    pe_row, w1bd, b1_row, w2bd, b2_row, w1pair = cw
    n, n_pages = page_table.shape
    n_new = new_kv.shape[2]
    assert n_pages % 2 == 0 and n_new <= CMP_STRIDE
    past_sub = n_pages * (PAGE_SIZE // CMP_STRIDE)
    rows = past_sub + 16
    sub_w = CMP_STRIDE * KV_COLS
    pe = pe_row.reshape(2, 2, sub_w)
    pe_hi = pe.astype(BF16)
    pe_lo = (pe - pe_hi.astype(F32)).astype(BF16)
    pe_rows = jnp.stack([pe_hi[:, 0], pe_lo[:, 0], pe_hi[:, 1], pe_lo[:, 1]], axis=1)
    new_row = jnp.pad(new_kv.reshape(2, n, 1, n_new * KV_COLS), ((0, 0), (0, 0), (0, 3), (0, sub_w - n_new * KV_COLS)))
    new_sub = jnp.concatenate(
        [new_row.astype(BF16), jnp.broadcast_to(pe_rows[:, None], (2, n, 4, sub_w)),
         jnp.zeros((2, n, 8, sub_w), BF16)], axis=2)
    w1cat = w1pair
    m = np.arange(256)
    perm = np.zeros((256, 256), np.float32)
    perm[m, (m % 16) * 16 + m // 16] = 1.0
    perm = jnp.asarray(perm, BF16)
    grid_spec = pltpu.PrefetchScalarGridSpec(
        num_scalar_prefetch=1,
        grid=(2, n),
        in_specs=[
            pl.BlockSpec(memory_space=pl.ANY), pl.BlockSpec(memory_space=pl.ANY),
            pl.BlockSpec((1, 1, 16, sub_w), lambda kv, i, pt: (kv, i, 0, 0)),
            pl.BlockSpec(perm.shape, lambda kv, i, pt: (0, 0)),
            pl.BlockSpec((1,) + w1cat.shape[1:], lambda kv, i, pt: (kv, 0, 0)),
            pl.BlockSpec((1,) + b1_row.shape[1:], lambda kv, i, pt: (kv, 0, 0)),
            pl.BlockSpec((1,) + w2bd.shape[1:], lambda kv, i, pt: (kv, 0, 0)),
            pl.BlockSpec((1,) + b2_row.shape[1:], lambda kv, i, pt: (kv, 0, 0)),
        ],
        out_specs=pl.BlockSpec((1, 1, out_rows, KV_COLS), lambda kv, i, pt: (kv, i, 0, 0)),
        scratch_shapes=[pltpu.VMEM((2, n_pages, KV_COLS, PAGE_SIZE), F32),
                        pltpu.VMEM((rows, sub_w), BF16),
                        pltpu.SemaphoreType.DMA((2,))],
    )
    return pl.pallas_call(
        functools.partial(_compress_paged_body, n_pages=n_pages, out_rows=out_rows),
        grid_spec=grid_spec,
        out_shape=jax.ShapeDtypeStruct((2, n, out_rows, KV_COLS), BF16),
        compiler_params=_params("arbitrary", "arbitrary"),
        name="compress_paged",
    )(page_table, pool_k, pool_v, new_sub, perm, w1cat, b1_row, w2bd, b2_row)


PAGES_PER_TILE = SEL_TK // PAGE_SIZE


def _nsa_sample_paged_body(pt_ref, pool_k, pool_v, q_ref, kcc_ref, vcc_ref, ksn_ref, vsn_ref,
                           kwc_ref, vwc_ref, kwn_ref, vwn_ref, gates_ref, ovt_ref, zb_ref, o_ref,
                           kbuf, vbuf, ksem, vsem, *, n_pages, n_new, win_buf):
    i = pl.program_id(0)
    n_seq = pl.num_programs(0)
    slot = i % 2
    past_len = n_pages * PAGE_SIZE

    def start(seq, slot_idx):
        _start_pages(pt_ref, seq, pool_k, kbuf, slot_idx, ksem, n_pages)
        _start_pages(pt_ref, seq, pool_v, vbuf, slot_idx, vsem, n_pages)

    @pl.when(i == 0)
    def _():
        start(i, slot)

    @pl.when(i + 1 < n_seq)
    def _():
        start(i + 1, 1 - slot)

    rows = N_HEADS * n_new
    grows = GROUP * n_new
    n_cmp = kcc_ref.shape[2]
    n_selp = ovt_ref.shape[0]
    qb = q_ref[0]
    row = lax.broadcasted_iota(jnp.int32, (rows, 1), 0)
    q_pos = past_len + row % n_new

    s = _dot_nt(qb, kcc_ref[0, 0])
    cmp_end = lax.broadcasted_iota(jnp.int32, (1, n_cmp), 1) * CMP_STRIDE + (CMP_BLOCK - 1)
    p_c = _softmax_rows(s, cmp_end <= q_pos)
    o_c = jnp.dot(p_c.astype(BF16), vcc_ref[0, 0], preferred_element_type=F32)

    parts = []
    for g in range(N_KV):
        acc = p_c[g * grows:g * grows + n_new]
        for r in range(1, GROUP):
            acc = acc + p_c[g * grows + r * n_new:g * grows + (r + 1) * n_new]
        parts.extend([acc] * GROUP)
    p_sum = jnp.concatenate(parts, axis=0)
    imp = _dot_nt(ovt_ref[...], p_sum, precision=lax.Precision.HIGHEST)
    jj = lax.broadcasted_iota(jnp.int32, (n_selp, rows), 0)
    cur = (past_len + lax.broadcasted_iota(jnp.int32, (n_selp, rows), 1) % n_new) // SEL_BLOCK
    forced = (jj == 0) | (jj == cur) | (jj == cur - 1)
    imp = jnp.where(forced, FORCE, imp)
    imp = jnp.where(jj <= cur, imp, NEG)
    sel = jnp.zeros((n_selp, rows), F32)
    n_sel = (past_len + n_new + SEL_BLOCK - 1) // SEL_BLOCK
    for _ in range(min(N_SEL, n_sel)):
        mx = jnp.max(imp, axis=0, keepdims=True)
        first = jnp.min(jnp.where(imp == mx, jj, n_selp), axis=0, keepdims=True)
        pick = jj == first
        sel = jnp.where(pick, 1.0, sel)
        imp = jnp.where(pick, BELOW_NEG, imp)
    qa = jnp.concatenate([qb, jnp.where(sel.T > 0.0, 0.0, NEG).astype(BF16)], axis=1)

    def attend(carry, kt, vt, k0, width, causal):
        m, l, acc = carry
        st = jnp.dot(qa, jnp.concatenate([kt, zb_ref[:, pl.ds(k0, width)]], axis=0), preferred_element_type=F32)
        if causal:
            key = k0 + lax.broadcasted_iota(jnp.int32, (1, width), 1)
            st = jnp.where(key <= q_pos, st, NEG)
        m_new = jnp.maximum(m, jnp.max(st, axis=-1, keepdims=True))
        alpha = jnp.exp(m - m_new)
        p = jnp.exp(st - m_new)
        l = alpha * l + jnp.sum(p, axis=-1, keepdims=True)
        acc = alpha * acc + _dot_nt(p.astype(BF16), vt)
        return m_new, l, acc

    _wait_pages(pool_k, kbuf, slot, ksem, n_pages)
    _wait_pages(pool_v, vbuf, slot, vsem, n_pages)

    def sel_tile(c, carry):
        pages = lambda buf: jnp.concatenate(
            [buf[slot, c * PAGES_PER_TILE + p].astype(BF16) for p in range(PAGES_PER_TILE)], axis=1)
        return attend(carry, pages(kbuf), pages(vbuf), pl.multiple_of(c * SEL_TK, SEL_TK), SEL_TK, False)

    carry = (jnp.full((rows, 1), NEG, F32), jnp.zeros((rows, 1), F32), jnp.zeros((rows, KV_COLS), F32))
    carry = lax.fori_loop(0, n_pages // PAGES_PER_TILE, sel_tile, carry, unroll=8)
    _, l_s, acc_s = attend(carry, ksn_ref[0], vsn_ref[0], past_len, ksn_ref.shape[2], True)
    o_s = acc_s / l_s

    kwt = jnp.concatenate([kwc_ref[0].astype(BF16), kwn_ref[0]], axis=1)
    vwt = jnp.concatenate([vwc_ref[0].astype(BF16), vwn_ref[0]], axis=1)
    sw = jnp.dot(qb, kwt, preferred_element_type=F32)
    kpos = past_len - win_buf + lax.broadcasted_iota(jnp.int32, (1, kwt.shape[1]), 1)
    dist = q_pos - kpos
    p_w = _softmax_rows(sw, (dist >= 0) & (dist < WINDOW) & (kpos >= 0))
    o_w = _dot_nt(p_w.astype(BF16), vwt)

    gt = gates_ref[0]
    o = gt[:, 0:1] * o_c + gt[:, 1:2] * o_s + gt[:, 2:3] * o_w
    diag = lax.broadcasted_iota(jnp.int32, (rows, KV_COLS), 1) // HEAD_DIM == row // grows
    o_ref[0] = jnp.where(diag, o, 0.0)


def _nsa_sample_paged(page_table, pool_k, pool_v, qbd, kvcc, ksn, vsn, kwc, vwc, kwn, vwn, gate_rows, *, n_new, win_buf):
    n, n_pages = page_table.shape
    n_cmp = kvcc.shape[2]
    n_sel = (n_pages * PAGE_SIZE + n_new + SEL_BLOCK - 1) // SEL_BLOCK
    n_selp = (n_sel + LANES - 1) // LANES * LANES
    assert n_pages % PAGES_PER_TILE == 0 and n_pages >= PAGES_PER_TILE
    ovt = jnp.asarray(_overlap_matrix(n_cmp, n_selp).T)
    n_keys = n_pages * PAGE_SIZE + ksn.shape[2]
    zb = jnp.asarray(np.arange(n_selp)[:, None] == np.arange(n_keys)[None, :] // SEL_BLOCK, BF16)
    per = lambda a: pl.BlockSpec((1,) + a.shape[1:], lambda i, pt: (i,) + (0,) * (a.ndim - 1))
    grid_spec = pltpu.PrefetchScalarGridSpec(
        num_scalar_prefetch=1,
        grid=(n,),
        in_specs=[
            pl.BlockSpec(memory_space=pl.ANY), pl.BlockSpec(memory_space=pl.ANY),
            per(qbd),
            pl.BlockSpec((1, 1) + kvcc.shape[2:], lambda i, pt: (0, i, 0, 0)),
            pl.BlockSpec((1, 1) + kvcc.shape[2:], lambda i, pt: (1, i, 0, 0)),
            per(ksn), per(vsn), per(kwc), per(vwc), per(kwn), per(vwn), per(gate_rows),
            pl.BlockSpec(ovt.shape, lambda i, pt: (0, 0)),
            pl.BlockSpec(zb.shape, lambda i, pt: (0, 0)),
        ],
        out_specs=pl.BlockSpec((1, N_HEADS * n_new, KV_COLS), lambda i, pt: (i, 0, 0)),
        scratch_shapes=[pltpu.VMEM((2, n_pages, KV_COLS, PAGE_SIZE), F32),
                        pltpu.VMEM((2, n_pages, KV_COLS, PAGE_SIZE), F32),
                        pltpu.SemaphoreType.DMA((2,)), pltpu.SemaphoreType.DMA((2,))],
    )
    return pl.pallas_call(
        functools.partial(_nsa_sample_paged_body, n_pages=n_pages, n_new=n_new, win_buf=win_buf),
        grid_spec=grid_spec,
        out_shape=jax.ShapeDtypeStruct((n, N_HEADS * n_new, KV_COLS), F32),
        compiler_params=_params("arbitrary"),
        name="nsa_sample_paged",
    )(page_table, pool_k, pool_v, qbd, kvcc, kvcc, ksn, vsn, kwc, vwc, kwn, vwn, gate_rows, ovt, zb)


def _merge_body(x_ref, on_ref, gu_ref, v_ref, sga_ref, sgb_ref, wsg_ref, bsg_ref,
                wpa_ref, wpb_ref, wo_ref, g1_ref, b1_ref, wr_ref, br_ref,
                h_ref, route_ref, *, alpha):
    tm = x_ref.shape[0]
    lane = lax.broadcasted_iota(jnp.int32, (CHUNK, LANES), 1)
    tri = lax.broadcasted_iota(jnp.int32, (CHUNK, CHUNK), 0) >= lax.broadcasted_iota(jnp.int32, (CHUNK, CHUNK), 1)
    zero = jnp.zeros((CHUNK, CHUNK), BF16)
    chunks = []
    for c in range(tm // CHUNK):
        rs = slice(c * CHUNK, (c + 1) * CHUNK)
        cols = []
        for p in range(A_GROUPS // 2):
            cs = slice(p * LANES, (p + 1) * LANES)
            v2 = v_ref[rs, cs].astype(BF16)
            lo = jnp.dot(jnp.where(tri, wsg_ref[0, 2 * p], zero), v2, preferred_element_type=F32)
            hi = jnp.dot(jnp.where(tri, wsg_ref[0, 2 * p + 1], zero), v2, preferred_element_type=F32)
            mixed = jnp.where(lane < A_GDIM, lo, hi) + bsg_ref[0, :, cs]
            cols.append(gu_ref[rs, cs] * mixed)
        chunks.append(jnp.concatenate(cols, axis=1))
    a_out = jnp.concatenate(chunks, axis=0).astype(BF16)
    m = (sga_ref[...] * jnp.dot(a_out, wpa_ref[...], preferred_element_type=F32)
         + sgb_ref[...] * jnp.dot(on_ref[...], wpb_ref[...], preferred_element_type=F32))
    pre = alpha * x_ref[...] + jnp.dot(m.astype(BF16), wo_ref[...], preferred_element_type=F32)
    h = _layer_norm_rows(pre, g1_ref[...], b1_ref[...])
    h_ref[...] = h

    lg = jnp.dot(h, wr_ref[...], preferred_element_type=F32, precision=lax.Precision.HIGHEST) + br_ref[...]
    col = lax.broadcasted_iota(jnp.int32, (tm, LANES), 1)
    is_g = col < N_GROUPS
    gl = jnp.where(is_g, lg, BELOW_NEG)
    gmax = jnp.max(gl, axis=1, keepdims=True)
    grp = jnp.min(jnp.where(gl == gmax, col, LANES), axis=1, keepdims=True)
    p_grp = 1.0 / jnp.sum(jnp.where(is_g, jnp.exp(gl - gmax), 0.0), axis=1, keepdims=True)
    e_idx = col - N_GROUPS
    in_grp = (e_idx >= 0) & (e_idx < N_EXPERTS) & (e_idx // EXPERTS_PER_GROUP == grp)
    el = jnp.where(in_grp, lg, BELOW_NEG)
    m1 = jnp.max(el, axis=1, keepdims=True)
    i1 = jnp.min(jnp.where(el == m1, col, LANES), axis=1, keepdims=True)
    el2 = jnp.where(col == i1, BELOW_NEG, el)
    m2 = jnp.max(el2, axis=1, keepdims=True)
    i2 = jnp.min(jnp.where(el2 == m2, col, LANES), axis=1, keepdims=True)
    e2 = jnp.exp(m2 - m1)
    w1 = p_grp / (1.0 + e2)
    w2 = p_grp * e2 / (1.0 + e2)
    route = jnp.where(col == 0, (i1 - N_GROUPS).astype(F32),
                      jnp.where(col == 1, (i2 - N_GROUPS).astype(F32),
                                jnp.where(col == 2, w1, jnp.where(col == 3, w2, 0.0))))
    route_ref[...] = route


def _merge(x, o_nsa, gu, v, sga, sgb, wsg, bsg, wpa, wpb, wo, g1, b1, wr, br, *, alpha, n_first):
    t = x.shape[0]
    tm = MERGE_TM
    assert t % tm == 0
    row = lambda n: pl.BlockSpec((tm, n), lambda i: (i, 0))
    full = lambda a: pl.BlockSpec(a.shape, lambda i: (0,) * a.ndim)
    pick = lambda a: pl.BlockSpec((1,) + a.shape[1:], lambda i: (jnp.where(i < n_first, 0, 1),) + (0,) * (a.ndim - 1))
    return pl.pallas_call(
        functools.partial(_merge_body, alpha=alpha),
        grid=(t // tm,),
        in_specs=[row(D_MODEL), row(D_MODEL), row(A_HALF), row(A_HALF), row(D_MODEL), row(D_MODEL),
                  pick(wsg), pick(bsg), full(wpa), full(wpb), full(wo), full(g1), full(b1), full(wr), full(br)],
        out_specs=[row(D_MODEL), row(LANES)],
        out_shape=[jax.ShapeDtypeStruct((t, D_MODEL), F32), jax.ShapeDtypeStruct((t, LANES), F32)],
        compiler_params=_params("parallel"),
        name="merge",
    )(x, o_nsa, gu, v, sga, sgb, wsg, bsg, wpa, wpb, wo, g1, b1, wr, br)


def _row_copy(h_ref, tok, xbuf, slot, r, sem):
    return pltpu.make_async_copy(h_ref.at[pl.ds(tok, 1)], xbuf.at[slot, pl.ds(r, 1)], sem.at[slot])


def _experts_body(be_ref, st_ref, nb_ref, h_ref, wg_ref, wu_ref, wd_ref, o_ref, xbuf, sem):
    i = pl.program_id(0)
    bm = xbuf.shape[1]
    slot = i % 2
    n_used = nb_ref[0]

    def start(blk, slot_idx):
        def body(r, c):
            _row_copy(h_ref, st_ref[blk * bm + r], xbuf, slot_idx, r, sem).start()
            return c
        lax.fori_loop(0, bm, body, 0, unroll=8)

    @pl.when((i == 0) & (n_used > 0))
    def _():
        start(i, slot)

    @pl.when(i + 1 < n_used)
    def _():
        start(i + 1, 1 - slot)

    @pl.when(i < n_used)
    def _():
        for r in range(bm):
            _row_copy(h_ref, 0, xbuf, slot, r, sem).wait()
        x = xbuf[slot].astype(BF16)
        gate = jnp.dot(x, wg_ref[0], preferred_element_type=F32)
        up = jnp.dot(x, wu_ref[0], preferred_element_type=F32)
        hid = (jax.nn.silu(gate) * up).astype(BF16)
        o_ref[...] = jnp.dot(hid, wd_ref[0], preferred_element_type=F32)

    @pl.when(i >= n_used)
    def _():
        o_ref[...] = jnp.zeros(o_ref.shape, F32)


def _experts(blk_exp, slot_tok, n_used, h, wg, wu, wd):
    n_slots = slot_tok.shape[0]
    bm = MOE_BM
    grid_spec = pltpu.PrefetchScalarGridSpec(
        num_scalar_prefetch=3,
        grid=(n_slots // bm,),
        in_specs=[
            pl.BlockSpec(memory_space=pl.ANY),
            pl.BlockSpec((1, D_MODEL, D_EXPERT), lambda i, be, st, nb: (be[i], 0, 0)),
            pl.BlockSpec((1, D_MODEL, D_EXPERT), lambda i, be, st, nb: (be[i], 0, 0)),
            pl.BlockSpec((1, D_EXPERT, D_MODEL), lambda i, be, st, nb: (be[i], 0, 0)),
        ],
        out_specs=pl.BlockSpec((bm, D_MODEL), lambda i, be, st, nb: (i, 0)),
        scratch_shapes=[pltpu.VMEM((2, bm, D_MODEL), F32), pltpu.SemaphoreType.DMA((2,))],
    )
    return pl.pallas_call(
        _experts_body,
        grid_spec=grid_spec,
        out_shape=jax.ShapeDtypeStruct((n_slots, D_MODEL), F32),
        compiler_params=_params("arbitrary"),
        name="experts",
    )(blk_exp, slot_tok, n_used, h, wg, wu, wd)


def _final_body(h_ref, y_ref, g_ref, b_ref, o_ref, *, alpha):
    o_ref[...] = _layer_norm_rows(alpha * h_ref[...] + y_ref[...], g_ref[...], b_ref[...])


def _final_norm(h, y, g, b, *, alpha):
    t = h.shape[0]
    tm = MERGE_TM
    row = pl.BlockSpec((tm, D_MODEL), lambda i: (i, 0))
    full = lambda a: pl.BlockSpec(a.shape, lambda i: (0,) * a.ndim)
    return pl.pallas_call(
        functools.partial(_final_body, alpha=alpha),
        grid=(t // tm,),
        in_specs=[row, row, full(g), full(b)],
        out_specs=row,
        out_shape=jax.ShapeDtypeStruct((t, D_MODEL), F32),
        compiler_params=_params("parallel"),
        name="final_norm",
    )(h, y, g, b)


def _moe_dispatch(route):
    t = route.shape[0]
    n_assign = t * TOP_K
    flat_e = route[:, 0:TOP_K].astype(jnp.int32).reshape(-1)
    onehot = (flat_e[:, None] == jnp.arange(N_EXPERTS, dtype=jnp.int32)[None, :]).astype(jnp.int32)
    csum = jnp.cumsum(onehot, axis=0)
    counts = csum[-1]
    rank = jnp.sum(onehot * csum, axis=1) - 1
    padded = (counts + MOE_BM - 1) // MOE_BM * MOE_BM
    pad_end = jnp.cumsum(padded)
    pad_start = pad_end - padded
    start = jnp.cumsum(counts) - counts
    dest = (jnp.sum(onehot * pad_start[None, :], axis=1) + rank).reshape(t, TOP_K)
    n_blocks = -(-n_assign // MOE_BM) + N_EXPERTS
    blk_start = jnp.arange(n_blocks, dtype=jnp.int32) * MOE_BM
    blk_exp = jnp.minimum(jnp.sum(pad_end[None, :] <= blk_start[:, None], axis=1), N_EXPERTS - 1).astype(jnp.int32)
    order = jnp.argsort(flat_e)
    e_slot = jnp.repeat(blk_exp, MOE_BM)
    off = jnp.arange(n_blocks * MOE_BM, dtype=jnp.int32) - pad_start[e_slot]
    src = jnp.clip(start[e_slot] + off, 0, n_assign - 1)
    slot_tok = jnp.where(off < counts[e_slot], order[src].astype(jnp.int32) // TOP_K, 0)
    n_used = (pad_end[-1:] // MOE_BM).astype(jnp.int32)
    return slot_tok, dest, blk_exp, n_used


def _nsa_sample_glue(page_table, cmp_k, cmp_v, sel_k, sel_v, win_k, win_v, q, kc, vc, ksb, vsb, kwb, vwb, gates, cw, *, n_new):
    n, n_pages = page_table.shape
    ts = n * n_new
    win_buf = win_k.shape[1]
    new = lambda a: a.reshape(n, n_new, KV_COLS)
    cmp_out = (n_pages * (PAGE_SIZE // CMP_STRIDE) + 16 + LANES - 1) // LANES * LANES
    kvcc = _compress_paged(page_table, _page_view(cmp_k), _page_view(cmp_v), jnp.stack([new(kc), new(vc)]), cw, out_rows=cmp_out)
    new_t = lambda a: jnp.pad(new(a).transpose(0, 2, 1), ((0, 0), (0, 0), (0, LANES - n_new)))
    q_s = q.reshape(n, n_new, N_KV, GROUP, HEAD_DIM)
    qbd = jnp.einsum('ntgrd,gj->ngrtjd', q_s, jnp.eye(N_KV, dtype=BF16)).reshape(n, N_HEADS * n_new, KV_COLS)
    g_s = gates.reshape(n, n_new, N_KV, LANES)[..., :3 * GROUP].reshape(n, n_new, N_KV, 3, GROUP)
    gate_rows = jnp.pad(g_s.transpose(0, 2, 4, 1, 3).reshape(n, N_HEADS * n_new, 3), ((0, 0), (0, 0), (0, LANES - 3)))
    od = _nsa_sample_paged(page_table, _page_view(sel_k), _page_view(sel_v), qbd, kvcc,
                           new_t(ksb), new_t(vsb), _page_view(win_k), _page_view(win_v),
                           new_t(kwb), new_t(vwb), gate_rows, n_new=n_new, win_buf=win_buf)
    o_s = jnp.einsum('ngrtgd->ntgrd', od.reshape(n, N_KV, GROUP, n_new, N_KV, HEAD_DIM))
    return o_s.reshape(ts, N_HEADS * HEAD_DIM).astype(BF16)


def _sgu_weights(sgu_w, sgu_b, n_new):
    reps = CHUNK // n_new
    w_short = jnp.einsum('ab,gts->gatbs', jnp.eye(reps, dtype=F32), sgu_w[:, :n_new, :n_new]).reshape(A_GROUPS, CHUNK, CHUNK)
    wsg = jnp.stack([sgu_w[:, :CHUNK, :CHUNK], w_short]).astype(BF16)
    b_full = jnp.repeat(sgu_b[:, :CHUNK].T, A_GDIM, axis=1)
    b_short = jnp.tile(jnp.repeat(sgu_b[:, :n_new].T, A_GDIM, axis=1), (reps, 1))
    return wsg, jnp.stack([b_full, b_short])


def kernel(x_prompt, x_sample, cache_cmp_k, cache_cmp_v, cache_sel_k, cache_sel_v, cache_win_k, cache_win_v, page_table, w_in, cmp_pe, cmp_w1, cmp_b1, cmp_w2, cmp_b2, sgu_ln_g, sgu_ln_b, sgu_w, sgu_b, w_proj_a, w_proj_b, w_out, ln1_g, ln1_b, router_group_w, router_group_b, router_expert_w, router_expert_b, exp_w_gate, exp_w_up, exp_w_down, ln2_g, ln2_b):
    depth = w_in.shape[0]
    alpha = (2.0 * depth) ** 0.25
    b, t, _ = x_prompt.shape
    n, n_new, _ = x_sample.shape
    tp, ts = b * t, n * n_new
    past_len = page_table.shape[1] * PAGE_SIZE
    win_buf = cache_win_k.shape[2]
    assert CHUNK % n_new == 0 and tp % MERGE_TM == 0 and ts % MERGE_TM == 0

    x_p, x_s = x_prompt, x_sample
    states = []
    for l in range(depth):
        x_all = jnp.concatenate([x_p.reshape(tp, D_MODEL), x_s.reshape(ts, D_MODEL)], axis=0)
        (q, kc, vc, ks, vs, kw, vw, ksb, vsb, kwb, vwb, gu, v, sga, sgb, gates) = _in_proj(
            x_all, _pack_w_in(w_in[l]), sgu_ln_g[l][None], sgu_ln_b[l][None])
        cw = _compress_weights(cmp_pe[l], cmp_w1[l], cmp_b1[l], cmp_w2[l], cmp_b2[l])

        sub_w = CMP_STRIDE * KV_COLS
        kcc, vcc = _compress(kc[:tp].reshape(b, t // CMP_STRIDE, sub_w), vc[:tp].reshape(b, t // CMP_STRIDE, sub_w),
                             cw, tile=128, out_rows=t // CMP_STRIDE)
        seq = lambda a: a[:tp].reshape(b, t, -1)
        o_p = _nsa_prompt_glue(seq(q), kcc, vcc, seq(ksb), seq(vsb), seq(kwb), seq(vwb), seq(gates))

        tail = lambda a: a[tp:]
        o_s = _nsa_sample_glue(page_table, cache_cmp_k[l], cache_cmp_v[l], cache_sel_k[l], cache_sel_v[l],
                               cache_win_k[l], cache_win_v[l], tail(q), tail(kc), tail(vc), tail(ksb), tail(vsb),
                               tail(kwb), tail(vwb), tail(gates), cw, n_new=n_new)
        o_all = jnp.concatenate([o_p.reshape(tp, -1), o_s], axis=0)

        wsg, bsg = _sgu_weights(sgu_w[l], sgu_b[l], n_new)
        wr = jnp.pad(jnp.concatenate([router_group_w[l], router_expert_w[l]], axis=1), ((0, 0), (0, LANES - N_GROUPS - N_EXPERTS)))
        br = jnp.pad(jnp.concatenate([router_group_b[l], router_expert_b[l]]), (0, LANES - N_GROUPS - N_EXPERTS))[None]
        h, route = _merge(x_all, o_all, gu, v, sga, sgb, wsg, bsg,
                              w_proj_a[l].astype(BF16), w_proj_b[l].astype(BF16), w_out[l].astype(BF16),
                              ln1_g[l][None], ln1_b[l][None], wr, br, alpha=alpha, n_first=tp // MERGE_TM)

        slot_tok, dest, blk_exp, n_used = _moe_dispatch(route)
        out = _experts(blk_exp, slot_tok, n_used, h,
                       exp_w_gate[l].astype(BF16), exp_w_up[l].astype(BF16), exp_w_down[l].astype(BF16))
        y_moe = route[:, 2:3] * out[dest[:, 0]] + route[:, 3:4] * out[dest[:, 1]]
        y_all = _final_norm(h, y_moe, ln2_g[l][None], ln2_b[l][None], alpha=alpha)

        kv5 = lambda a, lo, hi, bb: a[lo:hi].reshape(bb, -1, N_KV, HEAD_DIM)
        keep_p = min(WINDOW, t)
        prompt_state = tuple(kv5(a, 0, tp, b) for a in (kc, vc, ks, vs)) + tuple(
            kv5(a, 0, tp, b)[:, t - keep_p:] for a in (kw, vw))
        keep_s = min(WINDOW, win_buf + n_new)
        win_all = lambda cache, a: jnp.concatenate([cache[l], kv5(a, tp, tp + ts, n)], axis=1)[:, win_buf + n_new - keep_s:]
        sample_state = tuple(kv5(a, tp, tp + ts, n) for a in (kc, vc, ks, vs)) + (
            win_all(cache_win_k, kw), win_all(cache_win_v, vw), v[tp:].reshape(n, n_new, A_HALF))
        states.append(prompt_state + sample_state)
        x_p, x_s = y_all[:tp].reshape(b, t, D_MODEL), y_all[tp:].reshape(n, n_new, D_MODEL)
    return (x_p, x_s) + tuple(jnp.stack(z) for z in zip(*states))
```
